```python
import math
import jax, jax.numpy as jnp
from jax import lax
import numpy as np

D_MODEL = 2048
BATCH = 1
SEQ = 8192
DEPTH = 4

CHUNK = 64
N_A_LAYERS = DEPTH // 2
N_B_LAYERS = DEPTH - N_A_LAYERS
HEAD_DIM = 128
MEM_TOKENS = 256
MEM_HEADS = 4
MEM_W = MEM_HEADS * HEAD_DIM
MIX_W = D_MODEL - MEM_W
SGU_CHUNK = 128
SGU_GROUPS = 4
SGU_GROUP_W = MIX_W // SGU_GROUPS
DIFF_HEADS = MIX_W // (2 * HEAD_DIM)
DIFF_QK_W = 2 * DIFF_HEADS * HEAD_DIM
DIFF_V_DIM = 2 * HEAD_DIM
DIFF_BLOCK = 128
REL_BUCKETS = 32
REL_MAX_DIST = 128
N_GROUPS = 4
EXPERTS_PER_GROUP = 8
N_EXPERTS = N_GROUPS * EXPERTS_PER_GROUP
TOP_K = 2
EXPERT_FF = 512
MOE_BLOCK = 128
DN_ALPHA = (2 * DEPTH) ** 0.25
DN_BETA = (8 * DEPTH) ** -0.25
LN_EPS = 1e-5

kernel_name = 'hybrid_gmlp_diffattn_hmoe'


def layer_norm(x, g, b):
    xf = x.astype(jnp.float32)
    mu = jnp.mean(xf, -1, keepdims=True)
    var = jnp.mean(jnp.square(xf - mu), -1, keepdims=True)
    y = (xf - mu) * lax.rsqrt(var + LN_EPS)
    return (y * g.astype(jnp.float32) + b.astype(jnp.float32)).astype(x.dtype)


def rms_norm(x, g):
    xf = x.astype(jnp.float32)
    y = xf * lax.rsqrt(jnp.mean(jnp.square(xf), -1, keepdims=True) + LN_EPS)
    return (y * g.astype(jnp.float32)).astype(x.dtype)


def relative_bucket(rel):
    n = REL_BUCKETS // 2
    max_exact = n // 2
    ret = jnp.where(rel > 0, n, 0)
    a = jnp.abs(rel)
    af = jnp.maximum(a, 1).astype(jnp.float32)
    large = max_exact + (jnp.log(af / max_exact) / math.log(REL_MAX_DIST / max_exact)
                         * (n - max_exact)).astype(jnp.int32)
    large = jnp.minimum(large, n - 1)
    return ret + jnp.where(a < max_exact, a, large)


def spatial_gating(z, ln_g, ln_b, ws, bs):
    b_, s_, _ = z.shape
    u, v = jnp.split(z, 2, axis=-1)
    v = layer_norm(v, ln_g, ln_b)
    v = v.reshape(b_, s_ // SGU_CHUNK, SGU_CHUNK, SGU_GROUPS, SGU_GROUP_W)
    pos = jnp.arange(SGU_CHUNK)
    mask = (pos[None, :] // CHUNK) <= (pos[:, None] // CHUNK)
    w = jnp.where(mask[None], ws, jnp.zeros_like(ws))
    gate = jnp.einsum('gij,bnjgc->bnigc', w, v) + jnp.transpose(bs)[None, None, :, :, None]
    return u * gate.reshape(b_, s_, MIX_W)


def memory_attention(q, mem, w_kv):
    b_, s_, _ = q.shape
    kv = jnp.einsum('bmd,de->bme', mem, w_kv)
    k, v = jnp.split(kv, 2, axis=-1)
    q = q.reshape(b_, s_, MEM_HEADS, HEAD_DIM)
    k = k.reshape(b_, -1, MEM_HEADS, HEAD_DIM)
    v = v.reshape(b_, -1, MEM_HEADS, HEAD_DIM)
    s = jnp.einsum('bqhd,bkhd->bhqk', q, k).astype(jnp.float32) * (HEAD_DIM ** -0.5)
    p = jax.nn.softmax(s, axis=-1).astype(v.dtype)
    o = jnp.einsum('bhqk,bkhd->bqhd', p, v)
    return o.reshape(b_, s_, MEM_W)


def diff_attention(q, k, v, lam_params, subln_g, rel_bias, layer_idx):
    b_, s_, _ = q.shape
    lambda_init = 0.8 - 0.6 * math.exp(-0.3 * layer_idx)
    lp = lam_params.astype(jnp.float32)
    lam = jnp.exp(jnp.sum(lp[0] * lp[1])) - jnp.exp(jnp.sum(lp[2] * lp[3])) + lambda_init
    nqb = s_ // DIFF_BLOCK
    qb = q.reshape(b_, nqb, DIFF_BLOCK, DIFF_HEADS, 2, HEAD_DIM).transpose(1, 0, 2, 3, 4, 5)
    kpos = jnp.arange(s_)
    scale = HEAD_DIM ** -0.5

    def block(args):
        qblk, bi = args
        qpos = bi * DIFF_BLOCK + jnp.arange(DIFF_BLOCK)
        bias = rel_bias[relative_bucket(kpos[None, :] - qpos[:, None])]
        bias = jnp.transpose(bias, (2, 0, 1)).astype(jnp.float32)
        mask = (kpos[None, :] // CHUNK) <= (qpos[:, None] // CHUNK)
        s = jnp.einsum('bqhmd,bkhmd->bmhqk', qblk, k).astype(jnp.float32) * scale + bias
        s = jnp.where(mask, s, -jnp.inf)
        p = jax.nn.softmax(s, axis=-1)
        a = (p[:, 0] - lam * p[:, 1]).astype(v.dtype)
        return jnp.einsum('bhqk,bkhe->bqhe', a, v)

    o = lax.map(block, (qb, jnp.arange(nqb)))
    o = o.transpose(1, 0, 2, 3, 4).reshape(b_, s_, DIFF_HEADS, DIFF_V_DIM)
    o = rms_norm(o, subln_g) * (1.0 - lambda_init)
    return o.reshape(b_, s_, MIX_W)


def hierarchical_moe(x, wg1, bg1, wg2, bg2, w1, w3, w2):
    b_, s_, d_ = x.shape
    t = b_ * s_
    xt = x.reshape(t, d_)
    lg1 = jnp.einsum('td,dg->tg', xt, wg1).astype(jnp.float32) + bg1.astype(jnp.float32)
    pg = jax.nn.softmax(lg1, axis=-1)
    grp = jnp.argmax(lg1, axis=-1)
    p_sel = jnp.take_along_axis(pg, grp[:, None], axis=1)[:, 0]
    lg2_all = jnp.einsum('td,gde->tge', xt, wg2).astype(jnp.float32) + bg2.astype(jnp.float32)
    lg2 = jnp.take_along_axis(lg2_all, grp[:, None, None], axis=1)[:, 0]
    top_v, top_i = lax.top_k(lg2, TOP_K)
    gate = p_sel[:, None] * jax.nn.softmax(top_v, axis=-1)
    expert = grp[:, None] * EXPERTS_PER_GROUP + top_i
    n_assign = t * TOP_K
    flat_e = expert.reshape(-1)
    flat_w = gate.reshape(-1)
    flat_tok = jnp.repeat(jnp.arange(t, dtype=jnp.int32), TOP_K)
    order = jnp.argsort(flat_e)
    e_s, tok_s, w_s = flat_e[order], flat_tok[order], flat_w[order]
    counts = jnp.bincount(flat_e, length=N_EXPERTS)
    start = jnp.cumsum(counts) - counts
    padded = ((counts + MOE_BLOCK - 1) // MOE_BLOCK) * MOE_BLOCK
    pend = jnp.cumsum(padded)
    pstart = pend - padded
    dest = pstart[e_s] + (jnp.arange(n_assign) - start[e_s])
    n_blocks = n_assign // MOE_BLOCK + N_EXPERTS
    p_rows = n_blocks * MOE_BLOCK
    x_buf = jnp.zeros((p_rows, d_), x.dtype).at[dest].set(xt[tok_s])
    tok_buf = jnp.zeros((p_rows,), jnp.int32).at[dest].set(tok_s)
    w_buf = jnp.zeros((p_rows,), jnp.float32).at[dest].set(w_s)
    blk_e = jnp.minimum(jnp.searchsorted(pend, jnp.arange(n_blocks) * MOE_BLOCK, side='right'),
                        N_EXPERTS - 1)

    def expert_block(args):
        xb, e = args
        h = jax.nn.silu(xb @ w1[e]) * (xb @ w3[e])
        return h @ w2[e]

    yb = lax.map(expert_block, (x_buf.reshape(n_blocks, MOE_BLOCK, d_), blk_e)).reshape(p_rows, d_)
    y = jnp.zeros((t, d_), x.dtype).at[tok_buf].add(yb * w_buf[:, None].astype(yb.dtype))
    return y.reshape(b_, s_, d_)


def setup_inputs(seed: int = 0) -> dict:
    key = jax.random.key(seed)
    ks = jax.random.split(key, 24)
    f32 = jnp.float32

    def nrm(k, shape, scale):
        return jax.random.normal(k, shape, f32) * scale

    inv_d = D_MODEL ** -0.5
    return {
        'x': nrm(ks[0], (BATCH, SEQ, D_MODEL), 1.0),
        'mem': nrm(ks[1], (BATCH, MEM_TOKENS, D_MODEL), 1.0),
        'a_w_in': nrm(ks[2], (N_A_LAYERS, D_MODEL, 2 * MIX_W + MEM_W), inv_d),
        'a_sgu_ln_g': 1.0 + nrm(ks[3], (N_A_LAYERS, MIX_W), 0.02),
        'a_sgu_ln_b': nrm(ks[4], (N_A_LAYERS, MIX_W), 0.02),
        'a_ws': nrm(ks[5], (N_A_LAYERS, SGU_GROUPS, SGU_CHUNK, SGU_CHUNK), SGU_CHUNK ** -0.5),
        'a_bs': 1.0 + nrm(ks[6], (N_A_LAYERS, SGU_GROUPS, SGU_CHUNK), 0.1),
        'a_w_out': nrm(ks[7], (N_A_LAYERS, MIX_W + MEM_W, D_MODEL), (MIX_W + MEM_W) ** -0.5 * DN_BETA),
        'b_w_in': nrm(ks[8], (N_B_LAYERS, D_MODEL, DIFF_QK_W + MEM_W), inv_d),
        'b_lambda': nrm(ks[9], (N_B_LAYERS, 4, HEAD_DIM), 0.1),
        'b_subln_g': 1.0 + nrm(ks[10], (N_B_LAYERS, DIFF_V_DIM), 0.02),
        'b_w_out': nrm(ks[11], (N_B_LAYERS, MIX_W + MEM_W, D_MODEL), (MIX_W + MEM_W) ** -0.5 * DN_BETA),
        'shared_w_kv': nrm(ks[12], (D_MODEL, DIFF_QK_W + DIFF_HEADS * DIFF_V_DIM), inv_d),
        'rel_bias': nrm(ks[13], (REL_BUCKETS, DIFF_HEADS), 0.5),
        'mem_w_kv': nrm(ks[14], (DEPTH, D_MODEL, 2 * MEM_W), inv_d),
        'ln_g': 1.0 + nrm(ks[15], (DEPTH, 2, D_MODEL), 0.02),
        'ln_b': nrm(ks[16], (DEPTH, 2, D_MODEL), 0.02),
        'moe_wg1': nrm(ks[17], (DEPTH, D_MODEL, N_GROUPS), inv_d),
        'moe_bg1': nrm(ks[18], (DEPTH, N_GROUPS), 0.01),
        'moe_wg2': nrm(ks[19], (DEPTH, N_GROUPS, D_MODEL, EXPERTS_PER_GROUP), inv_d),
        'moe_bg2': nrm(ks[20], (DEPTH, N_GROUPS, EXPERTS_PER_GROUP), 0.01),
        'moe_w1': nrm(ks[21], (DEPTH, N_EXPERTS, D_MODEL, EXPERT_FF), inv_d),
        'moe_w3': nrm(ks[22], (DEPTH, N_EXPERTS, D_MODEL, EXPERT_FF), inv_d),
        'moe_w2': nrm(ks[23], (DEPTH, N_EXPERTS, EXPERT_FF, D_MODEL), EXPERT_FF ** -0.5 * DN_BETA),
    }


def reference(x, mem, a_w_in, a_sgu_ln_g, a_sgu_ln_b, a_ws, a_bs, a_w_out, b_w_in, b_lambda,
              b_subln_g, b_w_out, shared_w_kv, rel_bias, mem_w_kv, ln_g, ln_b, moe_wg1, moe_bg1,
              moe_wg2, moe_bg2, moe_w1, moe_w3, moe_w2):
    b_, s_, _ = x.shape
    shared_k = None
    shared_v = None
    for l in range(DEPTH):
        if l < N_A_LAYERS:
            i = l
            z = jnp.einsum('bsd,de->bse', x, a_w_in[i])
            z_mix, q_mem = z[..., :2 * MIX_W], z[..., 2 * MIX_W:]
            mix_out = spatial_gating(jax.nn.gelu(z_mix, approximate=False),
                                     a_sgu_ln_g[i], a_sgu_ln_b[i], a_ws[i], a_bs[i])
            w_out = a_w_out[i]
        else:
            i = l - N_A_LAYERS
            z = jnp.einsum('bsd,de->bse', x, b_w_in[i])
            q_diff, q_mem = z[..., :DIFF_QK_W], z[..., DIFF_QK_W:]
            mix_out = diff_attention(q_diff, shared_k, shared_v, b_lambda[i], b_subln_g[i], rel_bias, l)
            w_out = b_w_out[i]
        mem_out = memory_attention(q_mem, mem, mem_w_kv[l])
        t = jnp.einsum('bse,ed->bsd', jnp.concatenate([mix_out, mem_out], axis=-1), w_out)
        x = layer_norm(DN_ALPHA * x + t, ln_g[l, 0], ln_b[l, 0])
        f = hierarchical_moe(x, moe_wg1[l], moe_bg1[l], moe_wg2[l], moe_bg2[l],
                             moe_w1[l], moe_w3[l], moe_w2[l])
        x = layer_norm(DN_ALPHA * x + f, ln_g[l, 1], ln_b[l, 1])
        if l == N_A_LAYERS - 1:
            kv = jnp.einsum('bsd,de->bse', x, shared_w_kv)
            shared_k = kv[..., :DIFF_QK_W].reshape(b_, s_, DIFF_HEADS, 2, HEAD_DIM)
            shared_v = kv[..., DIFF_QK_W:].reshape(b_, s_, DIFF_HEADS, DIFF_V_DIM)
    return x
```

```python
import functools
import math

import jax
import jax.numpy as jnp
from jax import lax
from jax.experimental import pallas as pl
from jax.experimental.pallas import tpu as pltpu

F32 = jnp.float32
BF16 = jnp.bfloat16

D_MODEL = 2048
DEPTH = 4
CHUNK = 64
N_A_LAYERS = DEPTH // 2
HEAD_DIM = 128
MEM_TOKENS = 256
MEM_HEADS = 4
MEM_W = MEM_HEADS * HEAD_DIM
MIX_W = D_MODEL - MEM_W
SGU_CHUNK = 128
SGU_GROUPS = 4
SGU_GROUP_W = MIX_W // SGU_GROUPS
DIFF_HEADS = MIX_W // (2 * HEAD_DIM)
DIFF_QK_W = 2 * DIFF_HEADS * HEAD_DIM
DIFF_V_DIM = 2 * HEAD_DIM
REL_BUCKETS = 32
REL_MAX_DIST = 128
N_GROUPS = 4
EXPERTS_PER_GROUP = 8
N_EXPERTS = N_GROUPS * EXPERTS_PER_GROUP
TOP_K = 2
EXPERT_FF = 512
DN_ALPHA = (2 * DEPTH) ** 0.25
LN_EPS = 1e-5
QK_SCALE = HEAD_DIM ** -0.5
SQRT_HALF = math.sqrt(0.5)
NEG_BIG = -1e30

LANES = 128
VMEM_LIMIT_BYTES = 56 * 1024 * 1024

ROW_TILE = 256
ROUTER_TILE = 512
ATT_TILE = 256
MOE_TILE = 128
ROUTER_LANE0 = N_GROUPS


def _cparams(n_axes=1):
    return pltpu.CompilerParams(
        dimension_semantics=("arbitrary",) * n_axes,
        vmem_limit_bytes=VMEM_LIMIT_BYTES,
    )


def _dot(a, b):
    return jnp.dot(a, b, preferred_element_type=F32)


def _dot_nt(a, b):
    return lax.dot_general(a, b, (((1,), (1,)), ((), ())), preferred_element_type=F32)


def _gelu(x):
    return 0.5 * x * (1.0 + lax.erf(x * SQRT_HALF))


def _layer_norm(y, g, b):
    mu = jnp.mean(y, axis=-1, keepdims=True)
    d = y - mu
    var = jnp.mean(d * d, axis=-1, keepdims=True)
    return d * lax.rsqrt(var + LN_EPS) * g + b


def _resident(shape):
    nd = len(shape)
    return pl.BlockSpec(shape, lambda *_: (0,) * nd, pipeline_mode=pl.Buffered(1))


def _memkv_kernel(mem_ref, w_ref, kt_ref, v_ref):
    kv = _dot(mem_ref[...].astype(BF16), w_ref[0].astype(BF16))
    kt_ref[0] = kv[:, :MEM_W].T.astype(BF16)
    v_ref[0] = kv[:, MEM_W:].astype(BF16)


def _memkv(mem, mem_w_kv):
    n_layers = mem_w_kv.shape[0]
    m = mem.shape[0]
    return pl.pallas_call(
        _memkv_kernel,
        grid=(n_layers,),
        in_specs=[
            pl.BlockSpec((m, D_MODEL), lambda l: (0, 0)),
            pl.BlockSpec((1, D_MODEL, 2 * MEM_W), lambda l: (l, 0, 0)),
        ],
        out_specs=[
            pl.BlockSpec((1, MEM_W, m), lambda l: (l, 0, 0)),
            pl.BlockSpec((1, m, MEM_W), lambda l: (l, 0, 0)),
        ],
        out_shape=[
            jax.ShapeDtypeStruct((n_layers, MEM_W, m), BF16),
            jax.ShapeDtypeStruct((n_layers, m, MEM_W), BF16),
        ],
        compiler_params=_cparams(1),
        name="memkv",
    )(mem, mem_w_kv)


def _pre_a_kernel(x_ref, w_ref, lng_ref, lnb_ref, ws_ref, bst_ref, mix_ref, qm_ref, vn_ref):
    tm = x_ref.shape[0]
    x = x_ref[...].astype(BF16)
    v = _gelu(_dot(x, w_ref[:, MIX_W:2 * MIX_W]))
    vn_ref[...] = _layer_norm(v, lng_ref[...], lnb_ref[...]).astype(BF16)
    row = lax.broadcasted_iota(jnp.int32, (SGU_CHUNK, SGU_CHUNK), 0)
    col = lax.broadcasted_iota(jnp.int32, (SGU_CHUNK, SGU_CHUNK), 1)
    shift = CHUNK.bit_length() - 1
    keep = (col >> shift) <= (row >> shift)
    for g in range(SGU_GROUPS):
        cols = slice(g * SGU_GROUP_W, (g + 1) * SGU_GROUP_W)
        w_sp = jnp.where(keep, ws_ref[g], 0.0).astype(BF16)
        u = _gelu(_dot(x, w_ref[:, cols]))
        bias = bst_ref[:, g:g + 1]
        for c in range(tm // SGU_CHUNK):
            rows = slice(c * SGU_CHUNK, (c + 1) * SGU_CHUNK)
            gate = _dot(w_sp, vn_ref[rows, cols]) + bias
            mix_ref[rows, cols] = (u[rows] * gate).astype(BF16)
    qm_ref[...] = (_dot(x, w_ref[:, 2 * MIX_W:]) * QK_SCALE).astype(BF16)


def _pre_a(x, w_in, ln_g, ln_b, ws, bs_t):
    t = x.shape[0]
    n_in = w_in.shape[1]
    return pl.pallas_call(
        _pre_a_kernel,
        grid=(t // ROW_TILE,),
        in_specs=[
            pl.BlockSpec((ROW_TILE, D_MODEL), lambda i: (i, 0)),
            _resident((D_MODEL, n_in)),
            _resident((1, MIX_W)),
            _resident((1, MIX_W)),
            _resident((SGU_GROUPS, SGU_CHUNK, SGU_CHUNK)),
            _resident((SGU_CHUNK, SGU_GROUPS)),
        ],
        out_specs=[
            pl.BlockSpec((ROW_TILE, MIX_W), lambda i: (i, 0)),
            pl.BlockSpec((ROW_TILE, MEM_W), lambda i: (i, 0)),
        ],
        out_shape=[
            jax.ShapeDtypeStruct((t, MIX_W), BF16),
            jax.ShapeDtypeStruct((t, MEM_W), BF16),
        ],
        scratch_shapes=[pltpu.VMEM((ROW_TILE, MIX_W), BF16)],
        compiler_params=_cparams(1),
        name="pre_a",
    )(x, w_in, ln_g, ln_b, ws, bs_t)


def _pre_b_kernel(x_ref, w_ref, qd_ref, qm_ref):
    x = x_ref[...].astype(BF16)
    qd_ref[...] = (_dot(x, w_ref[:, :DIFF_QK_W]) * QK_SCALE).astype(BF16)
    qm_ref[...] = (_dot(x, w_ref[:, DIFF_QK_W:]) * QK_SCALE).astype(BF16)


def _pre_b(x, w_in):
    t = x.shape[0]
    return pl.pallas_call(
        _pre_b_kernel,
        grid=(t // ROW_TILE,),
        in_specs=[
            pl.BlockSpec((ROW_TILE, D_MODEL), lambda i: (i, 0)),
            _resident((D_MODEL, DIFF_QK_W + MEM_W)),
        ],
        out_specs=[
            pl.BlockSpec((ROW_TILE, DIFF_QK_W), lambda i: (i, 0)),
            pl.BlockSpec((ROW_TILE, MEM_W), lambda i: (i, 0)),
        ],
        out_shape=[
            jax.ShapeDtypeStruct((t, DIFF_QK_W), BF16),
            jax.ShapeDtypeStruct((t, MEM_W), BF16),
        ],
        compiler_params=_cparams(1),
        name="pre_b",
    )(x, w_in)


def _kvproj_kernel(x_ref, w_ref, k_ref, v_ref):
    x = x_ref[...].astype(BF16)
    k_ref[...] = _dot(x, w_ref[:, :DIFF_QK_W]).astype(BF16)
    v_ref[...] = _dot(x, w_ref[:, DIFF_QK_W:]).astype(BF16)


def _kvproj(x, w_kv):
    t = x.shape[0]
    n_v = DIFF_HEADS * DIFF_V_DIM
    return pl.pallas_call(
        _kvproj_kernel,
        grid=(t // ROW_TILE,),
        in_specs=[
            pl.BlockSpec((ROW_TILE, D_MODEL), lambda i: (i, 0)),
            _resident((D_MODEL, DIFF_QK_W + n_v)),
        ],
        out_specs=[
            pl.BlockSpec((ROW_TILE, DIFF_QK_W), lambda i: (i, 0)),
            pl.BlockSpec((ROW_TILE, n_v), lambda i: (i, 0)),
        ],
        out_shape=[
            jax.ShapeDtypeStruct((t, DIFF_QK_W), BF16),
            jax.ShapeDtypeStruct((t, n_v), BF16),
        ],
        compiler_params=_cparams(1),
        name="kvproj",
    )(x, w_kv)


def _attn_kernel(q_ref, k_ref, v_ref, bias_ref, lam_ref, g_ref, o_ref, acc1_ref, acc2_ref,
                 *, lambda_init):
    i = pl.program_id(1)
    tq = q_ref.shape[0]
    q = q_ref[...]
    q1 = q[:, :HEAD_DIM]
    q2 = q[:, HEAD_DIM:]
    acc1_ref[...] = jnp.zeros_like(acc1_ref)
    acc2_ref[...] = jnp.zeros_like(acc2_ref)

    def update(s, m, l, acc_ref, vblk):
        m_new = jnp.maximum(m, jnp.max(s, axis=-1, keepdims=True))
        alpha = jnp.exp(m - m_new)
        p = jnp.exp(s - m_new)
        l_new = alpha * l + jnp.sum(p, axis=-1, keepdims=True)
        acc_ref[...] = alpha * acc_ref[...] + _dot(p.astype(BF16), vblk)
        return m_new, l_new

    def block(j, bias, carry):
        m1, l1, m2, l2 = carry
        start = pl.multiple_of(j * tq, tq)
        kblk = k_ref[pl.ds(start, tq), :]
        vblk = v_ref[pl.ds(start, tq), :]
        s1 = _dot_nt(q1, kblk[:, :HEAD_DIM])
        s2 = _dot_nt(q2, kblk[:, HEAD_DIM:])
        if bias is not None:
            s1 = s1 + bias
            s2 = s2 + bias
        m1, l1 = update(s1, m1, l1, acc1_ref, vblk)
        m2, l2 = update(s2, m2, l2, acc2_ref, vblk)
        return m1, l1, m2, l2

    neg = jnp.full((tq, 1), NEG_BIG, F32)
    zero = jnp.zeros((tq, 1), F32)
    carry = (neg, zero, neg, zero)
    carry = lax.fori_loop(0, jnp.maximum(i - 1, 0), lambda j, c: block(j, None, c), carry)
    carry = lax.cond(i > 0,
                     lambda c: block(i - 1, bias_ref[0, :, :tq], c),
                     lambda c: c, carry)
    m1, l1, m2, l2 = block(i, bias_ref[0, :, tq:], carry)

    lp = lam_ref[...]
    lam = (jnp.exp(jnp.sum(lp[0:1] * lp[1:2], axis=-1, keepdims=True))
           - jnp.exp(jnp.sum(lp[2:3] * lp[3:4], axis=-1, keepdims=True)) + lambda_init)
    o = acc1_ref[...] / l1 - lam * (acc2_ref[...] / l2)
    ms = jnp.mean(o * o, axis=-1, keepdims=True)
    o = o * lax.rsqrt(ms + LN_EPS) * g_ref[...] * (1.0 - lambda_init)
    o_ref[...] = o.astype(o_ref.dtype)


def _attention(q, k, v, bias, lam_params, subln_g, lambda_init):
    t = q.shape[0]
    kern = functools.partial(_attn_kernel, lambda_init=lambda_init)
    return pl.pallas_call(
        kern,
        grid=(DIFF_HEADS, t // ATT_TILE),
        in_specs=[
            pl.BlockSpec((ATT_TILE, 2 * HEAD_DIM), lambda h, i: (i, h)),
            pl.BlockSpec((t, 2 * HEAD_DIM), lambda h, i: (0, h)),
            pl.BlockSpec((t, DIFF_V_DIM), lambda h, i: (0, h)),
            pl.BlockSpec((1, ATT_TILE, 2 * ATT_TILE), lambda h, i: (h, 0, 0)),
            pl.BlockSpec((4, HEAD_DIM), lambda h, i: (0, 0)),
            pl.BlockSpec((1, DIFF_V_DIM), lambda h, i: (0, 0)),
        ],
        out_specs=pl.BlockSpec((ATT_TILE, DIFF_V_DIM), lambda h, i: (i, h)),
        out_shape=jax.ShapeDtypeStruct((t, DIFF_HEADS * DIFF_V_DIM), BF16),
        scratch_shapes=[pltpu.VMEM((ATT_TILE, DIFF_V_DIM), F32),
                        pltpu.VMEM((ATT_TILE, DIFF_V_DIM), F32)],
        compiler_params=_cparams(2),
        name="diff_attn",
    )(q, k, v, bias, lam_params, subln_g)


def _relative_bucket(rel):
    n = REL_BUCKETS // 2
    max_exact = n // 2
    ret = jnp.where(rel > 0, n, 0)
    a = jnp.abs(rel)
    af = jnp.maximum(a, 1).astype(jnp.float32)
    large = max_exact + (jnp.log(af / max_exact) / math.log(REL_MAX_DIST / max_exact)
                         * (n - max_exact)).astype(jnp.int32)
    large = jnp.minimum(large, n - 1)
    return ret + jnp.where(a < max_exact, a, large)


def _near_bias_table(rel_bias):
    tq = ATT_TILE
    assert tq >= REL_MAX_DIST and tq % CHUNK == 0
    qpos = jnp.arange(tq)[:, None]
    kpos = jnp.arange(2 * tq)[None, :] - tq
    bias = rel_bias[_relative_bucket(kpos - qpos)].astype(F32)
    far = rel_bias[_relative_bucket(jnp.array(-2 * tq))].astype(F32)
    visible = (kpos // CHUNK) <= (qpos // CHUNK)
    table = jnp.where(visible[:, :, None], bias - far, NEG_BIG)
    return jnp.transpose(table, (2, 0, 1))


def _post_kernel(mix_ref, qm_ref, kt_ref, vm_ref, x_ref, w_ref, g_ref, b_ref, o_ref):
    heads = []
    for h in range(MEM_HEADS):
        cols = slice(h * HEAD_DIM, (h + 1) * HEAD_DIM)
        s = _dot(qm_ref[:, cols], kt_ref[0, cols, :])
        p = jnp.exp(s - jnp.max(s, axis=-1, keepdims=True))
        l = jnp.sum(p, axis=-1, keepdims=True)
        heads.append((_dot(p.astype(BF16), vm_ref[0, :, cols]) / l).astype(BF16))
    mem_out = jnp.concatenate(heads, axis=-1)
    t = _dot(mix_ref[...], w_ref[:MIX_W, :]) + _dot(mem_out, w_ref[MIX_W:, :])
    y = DN_ALPHA * x_ref[...] + t
    o_ref[...] = _layer_norm(y, g_ref[...], b_ref[...])


def _post(mix, qm, kt, vm, layer, x, w_out, ln_g, ln_b):
    t = x.shape[0]
    m = kt.shape[2]
    return pl.pallas_call(
        _post_kernel,
        grid=(t // ROW_TILE,),
        in_specs=[
            pl.BlockSpec((ROW_TILE, MIX_W), lambda i: (i, 0)),
            pl.BlockSpec((ROW_TILE, MEM_W), lambda i: (i, 0)),
            pl.BlockSpec((1, MEM_W, m), lambda i: (layer, 0, 0), pipeline_mode=pl.Buffered(1)),
            pl.BlockSpec((1, m, MEM_W), lambda i: (layer, 0, 0), pipeline_mode=pl.Buffered(1)),
            pl.BlockSpec((ROW_TILE, D_MODEL), lambda i: (i, 0)),
            _resident((D_MODEL, D_MODEL)),
            _resident((1, D_MODEL)),
            _resident((1, D_MODEL)),
        ],
        out_specs=pl.BlockSpec((ROW_TILE, D_MODEL), lambda i: (i, 0)),
        out_shape=jax.ShapeDtypeStruct((t, D_MODEL), F32),
        compiler_params=_cparams(1),
        name="post",
    )(mix, qm, kt, vm, x, w_out, ln_g, ln_b)


def _router_kernel(x_ref, w_ref, b_ref, info_ref, cnt_ref, carry_ref):
    step = pl.program_id(0)
    tm = x_ref.shape[0]

    @pl.when(step == 0)
    def _():
        carry_ref[...] = jnp.zeros_like(carry_ref)

    logits = jnp.dot(x_ref[...], w_ref[...], precision=lax.Precision.HIGHEST,
                     preferred_element_type=F32) + b_ref[...]
    lane = lax.broadcasted_iota(jnp.int32, (tm, LANES), 1).astype(F32)
    no_lane = float(LANES)

    def top(mask):
        val = jnp.max(jnp.where(mask, logits, -jnp.inf), axis=-1, keepdims=True)
        idx = jnp.min(jnp.where(mask & (logits == val), lane, no_lane), axis=-1, keepdims=True)
        return val, idx

    is_group = lane < float(N_GROUPS)
    g_val, g_idx = top(is_group)
    p_sel = 1.0 / jnp.sum(jnp.where(is_group, jnp.exp(logits - g_val), 0.0), axis=-1, keepdims=True)
    first = float(ROUTER_LANE0) + float(EXPERTS_PER_GROUP) * g_idx
    in_group = (lane >= first) & (lane < first + float(EXPERTS_PER_GROUP))
    v1, i1 = top(in_group)
    v2, i2 = top(in_group & (lane != i1))
    e2 = jnp.exp(v2 - v1)
    gate1 = p_sel / (1.0 + e2)
    gate2 = p_sel * e2 / (1.0 + e2)

    hit1 = lane == i1
    hit2 = lane == i2
    onehot = jnp.where(hit1 | hit2, 1.0, 0.0)
    r = lax.broadcasted_iota(jnp.int32, (tm, tm), 0)
    c = lax.broadcasted_iota(jnp.int32, (tm, tm), 1)
    strict_lower = jnp.where(c < r, 1.0, 0.0).astype(BF16)
    before = _dot(strict_lower, onehot.astype(BF16)) + carry_ref[0:1, :]
    rank1 = jnp.sum(jnp.where(hit1, before, 0.0), axis=-1, keepdims=True)
    rank2 = jnp.sum(jnp.where(hit2, before, 0.0), axis=-1, keepdims=True)
    total = carry_ref[0:1, :] + jnp.sum(onehot, axis=0, keepdims=True)
    carry_ref[...] = jnp.broadcast_to(total, carry_ref.shape)
    cnt_ref[...] = jnp.broadcast_to(total, cnt_ref.shape)

    vals = (i1 - float(ROUTER_LANE0), i2 - float(ROUTER_LANE0), rank1, rank2, gate1, gate2)
    info = jnp.zeros((tm, LANES), F32)
    for k, val in enumerate(vals):
        info = jnp.where(lane == float(k), val, info)
    info_ref[...] = info


def _router(x, w_r, b_r):
    t = x.shape[0]
    return pl.pallas_call(
        _router_kernel,
        grid=(t // ROUTER_TILE,),
        in_specs=[
            pl.BlockSpec((ROUTER_TILE, D_MODEL), lambda i: (i, 0)),
            _resident((D_MODEL, LANES)),
            _resident((1, LANES)),
        ],
        out_specs=[
            pl.BlockSpec((ROUTER_TILE, LANES), lambda i: (i, 0)),
            pl.BlockSpec((8, LANES), lambda i: (0, 0)),
        ],
        out_shape=[
            jax.ShapeDtypeStruct((t, LANES), F32),
            jax.ShapeDtypeStruct((8, LANES), F32),
        ],
        scratch_shapes=[pltpu.VMEM((8, LANES), F32)],
        compiler_params=_cparams(1),
        name="router",
    )(x, w_r, b_r)


def _expert_kernel(blk_e_ref, n_used_ref, tok_ref, x_hbm, w1_ref, w3_ref, w2_ref, y_ref,
                   xg_ref, sem, w1b_ref, w3b_ref, w2b_ref):
    b = pl.program_id(0)
    tm = xg_ref.shape[0]

    def row_copy(r):
        tok = tok_ref[0, 0, r]
        return pltpu.make_async_copy(x_hbm.at[pl.ds(tok, 1), :], xg_ref.at[pl.ds(r, 1), :], sem)

    @pl.when(b < n_used_ref[0])
    def _():
        def start(r, carry):
            row_copy(r).start()
            return carry
        lax.fori_loop(0, tm, start, 0)

        new_expert = jnp.logical_or(b == 0, blk_e_ref[b] != blk_e_ref[jnp.maximum(b - 1, 0)])

        @pl.when(new_expert)
        def _():
            w1b_ref[...] = w1_ref[0, 0].astype(BF16)
            w3b_ref[...] = w3_ref[0, 0].astype(BF16)
            w2b_ref[...] = w2_ref[0, 0].astype(BF16)

        def wait(r, carry):
            row_copy(r).wait()
            return carry
        lax.fori_loop(0, tm, wait, 0)

        xb = xg_ref[...].astype(BF16)
        h1 = _dot(xb, w1b_ref[...])
        h3 = _dot(xb, w3b_ref[...])
        h = (h1 * jax.nn.sigmoid(h1)) * h3
        y_ref[...] = _dot(h.astype(BF16), w2b_ref[...])

    @pl.when(b >= n_used_ref[0])
    def _():
        y_ref[...] = jnp.zeros_like(y_ref)


def _experts(x, tok_blocks, blk_e, n_used, layer, w1, w3, w2):
    n_blocks = tok_blocks.shape[0]
    grid_spec = pltpu.PrefetchScalarGridSpec(
        num_scalar_prefetch=2,
        grid=(n_blocks,),
        in_specs=[
            pl.BlockSpec((1, 1, MOE_TILE), lambda b, be, nu: (b, 0, 0), memory_space=pltpu.SMEM),
            pl.BlockSpec(memory_space=pl.ANY),
            pl.BlockSpec((1, 1, D_MODEL, EXPERT_FF), lambda b, be, nu: (layer, be[b], 0, 0)),
            pl.BlockSpec((1, 1, D_MODEL, EXPERT_FF), lambda b, be, nu: (layer, be[b], 0, 0)),
            pl.BlockSpec((1, 1, EXPERT_FF, D_MODEL), lambda b, be, nu: (layer, be[b], 0, 0)),
        ],
        out_specs=pl.BlockSpec((MOE_TILE, D_MODEL), lambda b, be, nu: (b, 0)),
        scratch_shapes=[
            pltpu.VMEM((MOE_TILE, D_MODEL), F32),
            pltpu.SemaphoreType.DMA(()),
            pltpu.VMEM((D_MODEL, EXPERT_FF), BF16),
            pltpu.VMEM((D_MODEL, EXPERT_FF), BF16),
            pltpu.VMEM((EXPERT_FF, D_MODEL), BF16),
        ],
    )
    return pl.pallas_call(
        _expert_kernel,
        grid_spec=grid_spec,
        out_shape=jax.ShapeDtypeStruct((n_blocks * MOE_TILE, D_MODEL), F32),
        compiler_params=_cparams(1),
        name="experts",
    )(blk_e, n_used, tok_blocks, x, w1, w3, w2)


def _combine_kernel(dest_ref, info_ref, x_ref, y_hbm, g_ref, b_ref, o_ref, r0_ref, r1_ref, sem):
    tm = x_ref.shape[0]

    def row_copies(r):
        d0 = dest_ref[0, 0, 2 * r]
        d1 = dest_ref[0, 0, 2 * r + 1]
        return (pltpu.make_async_copy(y_hbm.at[pl.ds(d0, 1), :], r0_ref.at[pl.ds(r, 1), :], sem),
                pltpu.make_async_copy(y_hbm.at[pl.ds(d1, 1), :], r1_ref.at[pl.ds(r, 1), :], sem))

    def start(r, carry):
        c0, c1 = row_copies(r)
        c0.start()
        c1.start()
        return carry
    lax.fori_loop(0, tm, start, 0)

    def wait(r, carry):
        c0, c1 = row_copies(r)
        c0.wait()
        c1.wait()
        return carry
    lax.fori_loop(0, tm, wait, 0)

    info = info_ref[...]
    gate1 = info[:, 4:5]
    gate2 = info[:, 5:6]
    f = gate1 * r0_ref[...] + gate2 * r1_ref[...]
    y = DN_ALPHA * x_ref[...] + f
    o_ref[...] = _layer_norm(y, g_ref[...], b_ref[...])


def _combine(dest_blocks, info, x, yb, ln_g, ln_b):
    t = x.shape[0]
    return pl.pallas_call(
        _combine_kernel,
        grid=(t // ROW_TILE,),
        in_specs=[
            pl.BlockSpec((1, 1, 2 * ROW_TILE), lambda i: (i, 0, 0), memory_space=pltpu.SMEM),
            pl.BlockSpec((ROW_TILE, LANES), lambda i: (i, 0)),
            pl.BlockSpec((ROW_TILE, D_MODEL), lambda i: (i, 0)),
            pl.BlockSpec(memory_space=pl.ANY),
            _resident((1, D_MODEL)),
            _resident((1, D_MODEL)),
        ],
        out_specs=pl.BlockSpec((ROW_TILE, D_MODEL), lambda i: (i, 0)),
        out_shape=jax.ShapeDtypeStruct((t, D_MODEL), F32),
        scratch_shapes=[
            pltpu.VMEM((ROW_TILE, D_MODEL), F32),
            pltpu.VMEM((ROW_TILE, D_MODEL), F32),
            pltpu.SemaphoreType.DMA(()),
        ],
        compiler_params=_cparams(1),
        name="combine",
    )(dest_blocks, info, x, yb, ln_g, ln_b)


def _moe(x, wg1, bg1, wg2, bg2, layer, w1, w3, w2, ln_g, ln_b):
    t = x.shape[0]
    w_r = jnp.concatenate([wg1, jnp.transpose(wg2, (1, 0, 2)).reshape(D_MODEL, N_EXPERTS)], axis=1)
    w_r = jnp.pad(w_r, ((0, 0), (0, LANES - w_r.shape[1])))
    b_r = jnp.pad(jnp.concatenate([bg1, bg2.reshape(-1)]), (0, LANES - N_GROUPS - N_EXPERTS))[None, :]
    info, cnt = _router(x, w_r, b_r)

    expert = info[:, 0:2].astype(jnp.int32)
    rank = info[:, 2:4].astype(jnp.int32)
    counts = cnt[0, ROUTER_LANE0:ROUTER_LANE0 + N_EXPERTS].astype(jnp.int32)
    padded = ((counts + MOE_TILE - 1) // MOE_TILE) * MOE_TILE
    pend = jnp.cumsum(padded)
    pstart = pend - padded
    dest = pstart[expert] + rank
    n_blocks = (t * TOP_K) // MOE_TILE + N_EXPERTS
    tok = jnp.repeat(jnp.arange(t, dtype=jnp.int32), TOP_K)
    tok_buf = jnp.zeros((n_blocks * MOE_TILE,), jnp.int32).at[dest.reshape(-1)].set(tok)
    blk_e = jnp.minimum(jnp.searchsorted(pend, jnp.arange(n_blocks) * MOE_TILE, side='right'),
                        N_EXPERTS - 1).astype(jnp.int32)
    n_used = (pend[-1:] // MOE_TILE).astype(jnp.int32)

    yb = _experts(x, tok_buf.reshape(n_blocks, 1, MOE_TILE), blk_e, n_used, layer, w1, w3, w2)
    dest_blocks = dest.reshape(t // ROW_TILE, 1, TOP_K * ROW_TILE)
    return _combine(dest_blocks, info, x, yb, ln_g, ln_b)


def kernel(x, mem, a_w_in, a_sgu_ln_g, a_sgu_ln_b, a_ws, a_bs, a_w_out, b_w_in, b_lambda, b_subln_g, b_w_out, shared_w_kv, rel_bias, mem_w_kv, ln_g, ln_b, moe_wg1, moe_bg1, moe_wg2, moe_bg2, moe_w1, moe_w3, moe_w2):
    b_, s_, d_ = x.shape
    assert b_ == 1 and d_ == D_MODEL
    h = x.reshape(s_, d_)
    kt_mem, v_mem = _memkv(mem.reshape(MEM_TOKENS, d_), mem_w_kv)
    bias_table = _near_bias_table(rel_bias)
    shared_k = shared_v = None
    for l in range(DEPTH):
        if l < N_A_LAYERS:
            i = l
            mix, qm = _pre_a(h, a_w_in[i].astype(BF16), a_sgu_ln_g[i][None, :], a_sgu_ln_b[i][None, :],
                             a_ws[i], jnp.transpose(a_bs[i]))
            w_out = a_w_out[i]
        else:
            i = l - N_A_LAYERS
            qd, qm = _pre_b(h, b_w_in[i].astype(BF16))
            lambda_init = 0.8 - 0.6 * math.exp(-0.3 * l)
            mix = _attention(qd, shared_k, shared_v, bias_table, b_lambda[i], b_subln_g[i][None, :],
                             lambda_init)
            w_out = b_w_out[i]
        h = _post(mix, qm, kt_mem, v_mem, l, h, w_out.astype(BF16), ln_g[l, 0][None, :], ln_b[l, 0][None, :])
        h = _moe(h, moe_wg1[l], moe_bg1[l], moe_wg2[l], moe_bg2[l], l, moe_w1, moe_w3, moe_w2,
                 ln_g[l, 1][None, :], ln_b[l, 1][None, :])
        if l == N_A_LAYERS - 1:
            shared_k, shared_v = _kvproj(h, shared_w_kv.astype(BF16))
    return h.reshape(b_, s_, d_)
```

```python
import functools
import math

import jax
import jax.numpy as jnp
from jax import lax
from jax.experimental import pallas as pl
from jax.experimental.pallas import tpu as pltpu

F32 = jnp.float32
BF16 = jnp.bfloat16

D_MODEL = 2048
DEPTH = 4
CHUNK = 64
N_A_LAYERS = DEPTH // 2
HEAD_DIM = 128
MEM_TOKENS = 256
MEM_HEADS = 4
MEM_W = MEM_HEADS * HEAD_DIM
MIX_W = D_MODEL - MEM_W
SGU_CHUNK = 128
SGU_GROUPS = 4
SGU_GROUP_W = MIX_W // SGU_GROUPS
DIFF_HEADS = MIX_W // (2 * HEAD_DIM)
DIFF_QK_W = 2 * DIFF_HEADS * HEAD_DIM
DIFF_V_DIM = 2 * HEAD_DIM
REL_BUCKETS = 32
REL_MAX_DIST = 128
N_GROUPS = 4
EXPERTS_PER_GROUP = 8
N_EXPERTS = N_GROUPS * EXPERTS_PER_GROUP
TOP_K = 2
EXPERT_FF = 512
DN_ALPHA = (2 * DEPTH) ** 0.25
LN_EPS = 1e-5
QK_SCALE = HEAD_DIM ** -0.5
SQRT_HALF = math.sqrt(0.5)
LOG2E = math.log2(math.e)
NEG_BIG = -1e30

LANES = 128
VMEM_LIMIT_BYTES = 56 * 1024 * 1024

ROW_TILE = 256
ROUTER_TILE = 512
ATT_TILE = 256
MOE_TILE = 128
ROUTER_LANE0 = N_GROUPS


def _cparams(n_axes=1):
    return pltpu.CompilerParams(
        dimension_semantics=("arbitrary",) * n_axes,
        vmem_limit_bytes=VMEM_LIMIT_BYTES,
    )


def _dot(a, b):
    return jnp.dot(a, b, preferred_element_type=F32)


def _dot_nt(a, b):
    return lax.dot_general(a, b, (((1,), (1,)), ((), ())), preferred_element_type=F32)


def _gelu(x):
    return 0.5 * x * (1.0 + lax.erf(x * SQRT_HALF))


def _layer_norm(y, g, b):
    mu = jnp.mean(y, axis=-1, keepdims=True)
    d = y - mu
    var = jnp.mean(d * d, axis=-1, keepdims=True)
    return d * lax.rsqrt(var + LN_EPS) * g + b


HI16 = 0xFFFF0000
PACK_W = D_MODEL // 2


def _pack_rows(y):
    lo = y[:, :PACK_W].astype(BF16).astype(F32)
    hi = y[:, PACK_W:].astype(BF16).astype(F32)
    return (pltpu.bitcast(lo, jnp.uint32) >> 16) | (pltpu.bitcast(hi, jnp.uint32) & jnp.uint32(HI16))


def _unpack_rows(w):
    lo = pltpu.bitcast(w << 16, F32)
    hi = pltpu.bitcast(w & jnp.uint32(HI16), F32)
    return jnp.concatenate([lo, hi], axis=-1)


def _resident(shape):
    nd = len(shape)
    return pl.BlockSpec(shape, lambda *_: (0,) * nd, pipeline_mode=pl.Buffered(1))


def _memkv_kernel(mem_ref, w_ref, kt_ref, v_ref):
    kv = _dot(mem_ref[...].astype(BF16), w_ref[0].astype(BF16))
    kt_ref[0] = kv[:, :MEM_W].T.astype(BF16)
    v_ref[0] = kv[:, MEM_W:].astype(BF16)


def _memkv(mem, mem_w_kv):
    n_layers = mem_w_kv.shape[0]
    m = mem.shape[0]
    return pl.pallas_call(
        _memkv_kernel,
        grid=(n_layers,),
        in_specs=[
            pl.BlockSpec((m, D_MODEL), lambda l: (0, 0)),
            pl.BlockSpec((1, D_MODEL, 2 * MEM_W), lambda l: (l, 0, 0)),
        ],
        out_specs=[
            pl.BlockSpec((1, MEM_W, m), lambda l: (l, 0, 0)),
            pl.BlockSpec((1, m, MEM_W), lambda l: (l, 0, 0)),
        ],
        out_shape=[
            jax.ShapeDtypeStruct((n_layers, MEM_W, m), BF16),
            jax.ShapeDtypeStruct((n_layers, m, MEM_W), BF16),
        ],
        compiler_params=_cparams(1),
        name="memkv",
    )(mem, mem_w_kv)


def _pre_a_kernel(x_ref, w_ref, lng_ref, lnb_ref, ws_ref, bst_ref, mix_ref, qm_ref, vn_ref):
    tm = x_ref.shape[0]
    x = x_ref[...].astype(BF16)
    v = _gelu(_dot(x, w_ref[:, MIX_W:2 * MIX_W]))
    vn_ref[...] = _layer_norm(v, lng_ref[...], lnb_ref[...]).astype(BF16)
    row = lax.broadcasted_iota(jnp.int32, (SGU_CHUNK, SGU_CHUNK), 0)
    col = lax.broadcasted_iota(jnp.int32, (SGU_CHUNK, SGU_CHUNK), 1)
    shift = CHUNK.bit_length() - 1
    keep = (col >> shift) <= (row >> shift)
    for g in range(SGU_GROUPS):
        cols = slice(g * SGU_GROUP_W, (g + 1) * SGU_GROUP_W)
        w_sp = jnp.where(keep, ws_ref[g], 0.0).astype(BF16)
        u = _gelu(_dot(x, w_ref[:, cols]))
        bias = bst_ref[:, g:g + 1]
        for c in range(tm // SGU_CHUNK):
            rows = slice(c * SGU_CHUNK, (c + 1) * SGU_CHUNK)
            gate = _dot(w_sp, vn_ref[rows, cols]) + bias
            mix_ref[rows, cols] = (u[rows] * gate).astype(BF16)
    qm_ref[...] = (_dot(x, w_ref[:, 2 * MIX_W:]) * QK_SCALE).astype(BF16)


def _pre_a(x, w_in, ln_g, ln_b, ws, bs_t):
    t = x.shape[0]
    n_in = w_in.shape[1]
    return pl.pallas_call(
        _pre_a_kernel,
        grid=(t // ROW_TILE,),
        in_specs=[
            pl.BlockSpec((ROW_TILE, D_MODEL), lambda i: (i, 0)),
            _resident((D_MODEL, n_in)),
            _resident((1, MIX_W)),
            _resident((1, MIX_W)),
            _resident((SGU_GROUPS, SGU_CHUNK, SGU_CHUNK)),
            _resident((SGU_CHUNK, SGU_GROUPS)),
        ],
        out_specs=[
            pl.BlockSpec((ROW_TILE, MIX_W), lambda i: (i, 0)),
            pl.BlockSpec((ROW_TILE, MEM_W), lambda i: (i, 0)),
        ],
        out_shape=[
            jax.ShapeDtypeStruct((t, MIX_W), BF16),
            jax.ShapeDtypeStruct((t, MEM_W), BF16),
        ],
        scratch_shapes=[pltpu.VMEM((ROW_TILE, MIX_W), BF16)],
        compiler_params=_cparams(1),
        name="pre_a",
    )(x, w_in, ln_g, ln_b, ws, bs_t)


def _pre_b_kernel(x_ref, wqt_ref, wm_ref, qt_ref, qm_ref):
    x = x_ref[...].astype(BF16)
    for h in range(DIFF_HEADS):
        rows = slice(h * 2 * HEAD_DIM, (h + 1) * 2 * HEAD_DIM)
        qt_ref[h, 0] = (_dot_nt(wqt_ref[rows, :], x) * (QK_SCALE * LOG2E)).astype(BF16)
    qm_ref[...] = (_dot(x, wm_ref[...]) * QK_SCALE).astype(BF16)


def _pre_b(x, w_q_t, w_mem):
    t = x.shape[0]
    return pl.pallas_call(
        _pre_b_kernel,
        grid=(t // ATT_TILE,),
        in_specs=[
            pl.BlockSpec((ATT_TILE, D_MODEL), lambda i: (i, 0)),
            _resident((DIFF_QK_W, D_MODEL)),
            _resident((D_MODEL, MEM_W)),
        ],
        out_specs=[
            pl.BlockSpec((DIFF_HEADS, 1, 2 * HEAD_DIM, ATT_TILE), lambda i: (0, i, 0, 0)),
            pl.BlockSpec((ATT_TILE, MEM_W), lambda i: (i, 0)),
        ],
        out_shape=[
            jax.ShapeDtypeStruct((DIFF_HEADS, t // ATT_TILE, 2 * HEAD_DIM, ATT_TILE), BF16),
            jax.ShapeDtypeStruct((t, MEM_W), BF16),
        ],
        compiler_params=_cparams(1),
        name="pre_b",
    )(x, w_q_t, w_mem)


def _kvproj_kernel(x_ref, wk_ref, wvt_ref, k_ref, vt_ref):
    x = x_ref[...].astype(BF16)
    k_ref[...] = _dot(x, wk_ref[...]).astype(BF16)
    for h in range(DIFF_HEADS):
        rows = slice(h * DIFF_V_DIM, (h + 1) * DIFF_V_DIM)
        vt_ref[h, 0] = _dot_nt(wvt_ref[rows, :], x).astype(BF16)


def _kvproj(x, w_k, w_v_t):
    t = x.shape[0]
    n_v = DIFF_HEADS * DIFF_V_DIM
    return pl.pallas_call(
        _kvproj_kernel,
        grid=(t // ATT_TILE,),
        in_specs=[
            pl.BlockSpec((ATT_TILE, D_MODEL), lambda i: (i, 0)),
            _resident((D_MODEL, DIFF_QK_W)),
            _resident((n_v, D_MODEL)),
        ],
        out_specs=[
            pl.BlockSpec((ATT_TILE, DIFF_QK_W), lambda i: (i, 0)),
            pl.BlockSpec((DIFF_HEADS, 1, DIFF_V_DIM, ATT_TILE), lambda i: (0, i, 0, 0)),
        ],
        out_shape=[
            jax.ShapeDtypeStruct((t, DIFF_QK_W), BF16),
            jax.ShapeDtypeStruct((DIFF_HEADS, t // ATT_TILE, DIFF_V_DIM, ATT_TILE), BF16),
        ],
        compiler_params=_cparams(1),
        name="kvproj",
    )(x, w_k, w_v_t)


def _attn_kernel(qt_ref, k_ref, vt_ref, bias_ref, lam_ref, g_ref, o_ref, acc1_ref, acc2_ref,
                 *, lambda_init):
    i = pl.program_id(1)
    tq = qt_ref.shape[3]
    q1t = qt_ref[0, 0, :HEAD_DIM, :]
    q2t = qt_ref[0, 0, HEAD_DIM:, :]
    acc1_ref[...] = jnp.zeros_like(acc1_ref)
    acc2_ref[...] = jnp.zeros_like(acc2_ref)

    def update(st, m, l, acc_ref, vts):
        m_new = jnp.maximum(m, jnp.max(st, axis=0, keepdims=True))
        alpha = jnp.exp2(m - m_new)
        p = jnp.exp2(st - m_new)
        l_new = alpha * l + jnp.sum(p, axis=0, keepdims=True)
        pb = p.astype(BF16)
        pv = _dot(vts[0], pb[:tq])
        for n in range(1, len(vts)):
            pv = pv + _dot(vts[n], pb[n * tq:(n + 1) * tq])
        acc_ref[...] = alpha * acc_ref[...] + pv
        return m_new, l_new

    def step(j0, n_blk, bias, carry):
        m1, l1, m2, l2 = carry
        start = pl.multiple_of(j0 * tq, tq)
        kblk = k_ref[pl.ds(start, n_blk * tq), :]
        vts = [vt_ref[0, j0 + n] for n in range(n_blk)]
        s1 = _dot(kblk[:, :HEAD_DIM], q1t)
        s2 = _dot(kblk[:, HEAD_DIM:], q2t)
        if bias is not None:
            s1 = s1 + bias
            s2 = s2 + bias
        m1, l1 = update(s1, m1, l1, acc1_ref, vts)
        m2, l2 = update(s2, m2, l2, acc2_ref, vts)
        return m1, l1, m2, l2

    neg = jnp.full((1, tq), NEG_BIG, F32)
    zero = jnp.zeros((1, tq), F32)
    carry = (neg, zero, neg, zero)
    n_far = jnp.maximum(i - 1, 0)
    n_pairs = n_far // 2
    carry = lax.fori_loop(0, n_pairs, lambda jj, c: step(2 * jj, 2, None, c), carry)
    carry = lax.cond(n_far > 2 * n_pairs, lambda c: step(i - 2, 1, None, c), lambda c: c, carry)
    carry = lax.cond(i > 0, lambda c: step(i - 1, 1, bias_ref[0, :tq, :], c), lambda c: c, carry)
    m1, l1, m2, l2 = step(i, 1, bias_ref[0, tq:, :], carry)

    lp = lam_ref[...]
    lam = (jnp.exp(jnp.sum(lp[0:1] * lp[1:2], axis=-1, keepdims=True))
           - jnp.exp(jnp.sum(lp[2:3] * lp[3:4], axis=-1, keepdims=True)) + lambda_init)
    ot = acc1_ref[...] / l1 - lam * (acc2_ref[...] / l2)
    ms = jnp.mean(ot * ot, axis=0, keepdims=True)
    ot = ot * lax.rsqrt(ms + LN_EPS)
    o_ref[...] = (ot.T * (g_ref[...] * (1.0 - lambda_init))).astype(o_ref.dtype)


def _attention(qt, k, vt, bias, lam_params, subln_g, lambda_init):
    t = k.shape[0]
    n_blk = t // ATT_TILE
    kern = functools.partial(_attn_kernel, lambda_init=lambda_init)
    return pl.pallas_call(
        kern,
        grid=(DIFF_HEADS, n_blk),
        in_specs=[
            pl.BlockSpec((1, 1, 2 * HEAD_DIM, ATT_TILE), lambda h, i: (h, i, 0, 0)),
            pl.BlockSpec((t, 2 * HEAD_DIM), lambda h, i: (0, h)),
            pl.BlockSpec((1, n_blk, DIFF_V_DIM, ATT_TILE), lambda h, i: (h, 0, 0, 0)),
            pl.BlockSpec((1, 2 * ATT_TILE, ATT_TILE), lambda h, i: (h, 0, 0)),
            pl.BlockSpec((4, HEAD_DIM), lambda h, i: (0, 0)),
            pl.BlockSpec((1, DIFF_V_DIM), lambda h, i: (0, 0)),
        ],
        out_specs=pl.BlockSpec((ATT_TILE, DIFF_V_DIM), lambda h, i: (i, h)),
        out_shape=jax.ShapeDtypeStruct((t, DIFF_HEADS * DIFF_V_DIM), BF16),
        scratch_shapes=[pltpu.VMEM((DIFF_V_DIM, ATT_TILE), F32),
                        pltpu.VMEM((DIFF_V_DIM, ATT_TILE), F32)],
        compiler_params=_cparams(2),
        name="diff_attn",
    )(qt, k, vt, bias, lam_params, subln_g)


def _relative_bucket(rel):
    n = REL_BUCKETS // 2
    max_exact = n // 2
    ret = jnp.where(rel > 0, n, 0)
    a = jnp.abs(rel)
    af = jnp.maximum(a, 1).astype(jnp.float32)
    large = max_exact + (jnp.log(af / max_exact) / math.log(REL_MAX_DIST / max_exact)
                         * (n - max_exact)).astype(jnp.int32)
    large = jnp.minimum(large, n - 1)
    return ret + jnp.where(a < max_exact, a, large)


def _near_bias_table(rel_bias):
    tq = ATT_TILE
    assert tq >= REL_MAX_DIST and tq % CHUNK == 0
    qpos = jnp.arange(tq)[:, None]
    kpos = jnp.arange(2 * tq)[None, :] - tq
    onehot = jax.nn.one_hot(_relative_bucket(kpos - qpos), REL_BUCKETS, dtype=F32)
    bias = jnp.einsum('qkb,bh->qkh', onehot, rel_bias.astype(F32),
                      precision=lax.Precision.HIGHEST)
    far = rel_bias[_relative_bucket(jnp.array(-2 * tq))].astype(F32)
    visible = (kpos // CHUNK) <= (qpos // CHUNK)
    table = jnp.where(visible[:, :, None], (bias - far) * LOG2E, NEG_BIG)
    return jnp.transpose(table, (2, 1, 0))


def _post_kernel(mix_ref, qm_ref, kt_ref, vm_ref, x_ref, w_ref, g_ref, b_ref, o_ref, op_ref):
    heads = []
    for h in range(MEM_HEADS):
        cols = slice(h * HEAD_DIM, (h + 1) * HEAD_DIM)
        s = _dot(qm_ref[:, cols], kt_ref[0, cols, :])
        p = jnp.exp(s - jnp.max(s, axis=-1, keepdims=True))
        l = jnp.sum(p, axis=-1, keepdims=True)
        heads.append((_dot(p.astype(BF16), vm_ref[0, :, cols]) / l).astype(BF16))
    mem_out = jnp.concatenate(heads, axis=-1)
    t = _dot(mix_ref[...], w_ref[:MIX_W, :]) + _dot(mem_out, w_ref[MIX_W:, :])
    y = _layer_norm(DN_ALPHA * x_ref[...] + t, g_ref[...], b_ref[...])
    o_ref[...] = y
    op_ref[...] = _pack_rows(y)


def _post(mix, qm, kt, vm, layer, x, w_out, ln_g, ln_b):
    t = x.shape[0]
    m = kt.shape[2]
    return pl.pallas_call(
        _post_kernel,
        grid=(t // ROW_TILE,),
        in_specs=[
            pl.BlockSpec((ROW_TILE, MIX_W), lambda i: (i, 0)),
            pl.BlockSpec((ROW_TILE, MEM_W), lambda i: (i, 0)),
            pl.BlockSpec((1, MEM_W, m), lambda i: (layer, 0, 0), pipeline_mode=pl.Buffered(1)),
            pl.BlockSpec((1, m, MEM_W), lambda i: (layer, 0, 0), pipeline_mode=pl.Buffered(1)),
            pl.BlockSpec((ROW_TILE, D_MODEL), lambda i: (i, 0)),
            _resident((D_MODEL, D_MODEL)),
            _resident((1, D_MODEL)),
            _resident((1, D_MODEL)),
        ],
        out_specs=[
            pl.BlockSpec((ROW_TILE, D_MODEL), lambda i: (i, 0)),
            pl.BlockSpec((ROW_TILE, PACK_W), lambda i: (i, 0)),
        ],
        out_shape=[
            jax.ShapeDtypeStruct((t, D_MODEL), F32),
            jax.ShapeDtypeStruct((t, PACK_W), jnp.uint32),
        ],
        compiler_params=_cparams(1),
        name="post",
    )(mix, qm, kt, vm, x, w_out, ln_g, ln_b)


def _router_kernel(x_ref, w_ref, b_ref, info_ref, cnt_ref, carry_ref):
    step = pl.program_id(0)
    tm = x_ref.shape[0]

    @pl.when(step == 0)
    def _():
        carry_ref[...] = jnp.zeros_like(carry_ref)

    logits = jnp.dot(x_ref[...], w_ref[...], precision=lax.Precision.HIGHEST,
                     preferred_element_type=F32) + b_ref[...]
    lane = lax.broadcasted_iota(jnp.int32, (tm, LANES), 1).astype(F32)
    no_lane = float(LANES)

    def top(mask):
        val = jnp.max(jnp.where(mask, logits, -jnp.inf), axis=-1, keepdims=True)
        idx = jnp.min(jnp.where(mask & (logits == val), lane, no_lane), axis=-1, keepdims=True)
        return val, idx

    is_group = lane < float(N_GROUPS)
    g_val, g_idx = top(is_group)
    p_sel = 1.0 / jnp.sum(jnp.where(is_group, jnp.exp(logits - g_val), 0.0), axis=-1, keepdims=True)
    first = float(ROUTER_LANE0) + float(EXPERTS_PER_GROUP) * g_idx
    in_group = (lane >= first) & (lane < first + float(EXPERTS_PER_GROUP))
    v1, i1 = top(in_group)
    v2, i2 = top(in_group & (lane != i1))
    e2 = jnp.exp(v2 - v1)
    gate1 = p_sel / (1.0 + e2)
    gate2 = p_sel * e2 / (1.0 + e2)

    hit1 = lane == i1
    hit2 = lane == i2
    onehot = jnp.where(hit1 | hit2, 1.0, 0.0)
    r = lax.broadcasted_iota(jnp.int32, (tm, tm), 0)
    c = lax.broadcasted_iota(jnp.int32, (tm, tm), 1)
    strict_lower = jnp.where(c < r, 1.0, 0.0).astype(BF16)
    before = _dot(strict_lower, onehot.astype(BF16)) + carry_ref[0:1, :]
    rank1 = jnp.sum(jnp.where(hit1, before, 0.0), axis=-1, keepdims=True)
    rank2 = jnp.sum(jnp.where(hit2, before, 0.0), axis=-1, keepdims=True)
    total = carry_ref[0:1, :] + jnp.sum(onehot, axis=0, keepdims=True)
    carry_ref[...] = jnp.broadcast_to(total, carry_ref.shape)
    cnt_ref[...] = jnp.broadcast_to(total, cnt_ref.shape)

    vals = (i1 - float(ROUTER_LANE0), i2 - float(ROUTER_LANE0), rank1, rank2, gate1, gate2)
    info = jnp.zeros((tm, LANES), F32)
    for k, val in enumerate(vals):
        info = jnp.where(lane == float(k), val, info)
    info_ref[...] = info


def _router(x, w_r, b_r):
    t = x.shape[0]
    return pl.pallas_call(
        _router_kernel,
        grid=(t // ROUTER_TILE,),
        in_specs=[
            pl.BlockSpec((ROUTER_TILE, D_MODEL), lambda i: (i, 0)),
            _resident((D_MODEL, LANES)),
            _resident((1, LANES)),
        ],
        out_specs=[
            pl.BlockSpec((ROUTER_TILE, LANES), lambda i: (i, 0)),
            pl.BlockSpec((8, LANES), lambda i: (0, 0)),
        ],
        out_shape=[
            jax.ShapeDtypeStruct((t, LANES), F32),
            jax.ShapeDtypeStruct((8, LANES), F32),
        ],
        scratch_shapes=[pltpu.VMEM((8, LANES), F32)],
        compiler_params=_cparams(1),
        name="router",
    )(x, w_r, b_r)


DISPATCH_TILE = 512
DMA_UNROLL = 8


def _dispatch_kernel(cnt_ref, pstart_ref, n_used_ref, dest_ref, xp_hbm, xbuf_hbm, zero_ref, sem, zsem):
    step = pl.program_id(0)
    n_tok = dest_ref.shape[2] // TOP_K
    n_blocks = xbuf_hbm.shape[0] // MOE_TILE

    def token_copies(r):
        tok = step * n_tok + r
        return [pltpu.make_async_copy(xp_hbm.at[pl.ds(tok, 1), :],
                                      xbuf_hbm.at[pl.ds(dest_ref[0, 0, TOP_K * r + k], 1), :], sem)
                for k in range(TOP_K)]

    def pad_rows(e):
        first = pstart_ref[e] + cnt_ref[e]
        n_pad = (-cnt_ref[e]) & (MOE_TILE - 1)
        return first, n_pad

    def zero_copy(row):
        return pltpu.make_async_copy(zero_ref.at[pl.ds(0, 1), :], xbuf_hbm.at[pl.ds(row, 1), :], zsem)

    def zero_block_copy(b):
        return pltpu.make_async_copy(zero_ref, xbuf_hbm.at[pl.ds(b * MOE_TILE, MOE_TILE), :], zsem)

    @pl.when(step == 0)
    def _():
        zero_ref[...] = jnp.zeros_like(zero_ref)

        def start_expert(e, carry):
            first, n_pad = pad_rows(e)
            lax.fori_loop(0, n_pad, lambda r, c: (zero_copy(first + r).start(), c)[1], 0)
            return carry
        lax.fori_loop(0, N_EXPERTS, start_expert, 0)
        lax.fori_loop(n_used_ref[0], n_blocks, lambda b, c: (zero_block_copy(b).start(), c)[1], 0)

    def start(g, carry):
        for u in range(DMA_UNROLL):
            for cp in token_copies(g * DMA_UNROLL + u):
                cp.start()
        return carry
    lax.fori_loop(0, n_tok // DMA_UNROLL, start, 0)

    def wait(g, carry):
        for u in range(DMA_UNROLL):
            for cp in token_copies(g * DMA_UNROLL + u):
                cp.wait()
        return carry
    lax.fori_loop(0, n_tok // DMA_UNROLL, wait, 0)

    @pl.when(step == 0)
    def _():
        def wait_expert(e, carry):
            first, n_pad = pad_rows(e)
            lax.fori_loop(0, n_pad, lambda r, c: (zero_copy(first + r).wait(), c)[1], 0)
            return carry
        lax.fori_loop(0, N_EXPERTS, wait_expert, 0)
        lax.fori_loop(n_used_ref[0], n_blocks, lambda b, c: (zero_block_copy(b).wait(), c)[1], 0)


def _dispatch(counts, pstart, n_used, dest, xp, n_rows):
    t = xp.shape[0]
    assert MOE_TILE & (MOE_TILE - 1) == 0 and t % DISPATCH_TILE == 0 and DISPATCH_TILE % DMA_UNROLL == 0
    dest_blocks = dest.reshape(t // DISPATCH_TILE, 1, TOP_K * DISPATCH_TILE)
    grid_spec = pltpu.PrefetchScalarGridSpec(
        num_scalar_prefetch=3,
        grid=(t // DISPATCH_TILE,),
        in_specs=[
            pl.BlockSpec((1, 1, TOP_K * DISPATCH_TILE), lambda i, c, p, n: (i, 0, 0), memory_space=pltpu.SMEM),
            pl.BlockSpec(memory_space=pl.ANY),
        ],
        out_specs=pl.BlockSpec(memory_space=pl.ANY),
        scratch_shapes=[
            pltpu.VMEM((MOE_TILE, PACK_W), jnp.uint32),
            pltpu.SemaphoreType.DMA(()),
            pltpu.SemaphoreType.DMA(()),
        ],
    )
    return pl.pallas_call(
        _dispatch_kernel,
        grid_spec=grid_spec,
        out_shape=jax.ShapeDtypeStruct((n_rows, PACK_W), jnp.uint32),
        compiler_params=_cparams(1),
        name="dispatch",
    )(counts, pstart, n_used, dest_blocks, xp)


def _expert_kernel(blk_e_ref, n_used_ref, x_ref, w1_ref, w3_ref, w2_ref, y_ref,
                   w1b_ref, w3b_ref, w2b_ref):
    b = pl.program_id(0)

    @pl.when(b < n_used_ref[0])
    def _():
        new_expert = jnp.logical_or(b == 0, blk_e_ref[b] != blk_e_ref[jnp.maximum(b - 1, 0)])

        @pl.when(new_expert)
        def _():
            w1b_ref[...] = w1_ref[0, 0].astype(BF16)
            w3b_ref[...] = w3_ref[0, 0].astype(BF16)
            w2b_ref[...] = w2_ref[0, 0].astype(BF16)

        xb = _unpack_rows(x_ref[...]).astype(BF16)
        h1 = _dot(xb, w1b_ref[...])
        h3 = _dot(xb, w3b_ref[...])
        h = (h1 * jax.nn.sigmoid(h1)) * h3
        y_ref[...] = _pack_rows(_dot(h.astype(BF16), w2b_ref[...]))

    @pl.when(b >= n_used_ref[0])
    def _():
        y_ref[...] = jnp.zeros_like(y_ref)


def _experts(xbuf, blk_e, n_used, layer, w1, w3, w2):
    n_blocks = xbuf.shape[0] // MOE_TILE

    def x_index(b, be, nu):
        return (jnp.minimum(b, jnp.maximum(nu[0] - 1, 0)), 0)

    grid_spec = pltpu.PrefetchScalarGridSpec(
        num_scalar_prefetch=2,
        grid=(n_blocks,),
        in_specs=[
            pl.BlockSpec((MOE_TILE, PACK_W), x_index),
            pl.BlockSpec((1, 1, D_MODEL, EXPERT_FF), lambda b, be, nu: (layer, be[b], 0, 0)),
            pl.BlockSpec((1, 1, D_MODEL, EXPERT_FF), lambda b, be, nu: (layer, be[b], 0, 0)),
            pl.BlockSpec((1, 1, EXPERT_FF, D_MODEL), lambda b, be, nu: (layer, be[b], 0, 0)),
        ],
        out_specs=pl.BlockSpec((MOE_TILE, PACK_W), lambda b, be, nu: (b, 0)),
        scratch_shapes=[
            pltpu.VMEM((D_MODEL, EXPERT_FF), BF16),
            pltpu.VMEM((D_MODEL, EXPERT_FF), BF16),
            pltpu.VMEM((EXPERT_FF, D_MODEL), BF16),
        ],
    )
    return pl.pallas_call(
        _expert_kernel,
        grid_spec=grid_spec,
        out_shape=jax.ShapeDtypeStruct(xbuf.shape, jnp.uint32),
        compiler_params=_cparams(1),
        name="experts",
    )(blk_e, n_used, xbuf, w1, w3, w2)


def _combine_kernel(dest_ref, dest_next_ref, info_ref, x_ref, y_hbm, g_ref, b_ref, o_ref, rows_ref, sem):
    i = pl.program_id(0)
    n_steps = pl.num_programs(0)
    tm = x_ref.shape[0]
    slot = lax.rem(i, 2)

    def row_copies(idx_ref, s, r):
        return [pltpu.make_async_copy(y_hbm.at[pl.ds(idx_ref[0, 0, TOP_K * r + k], 1), :],
                                      rows_ref.at[s, k, pl.ds(r, 1), :], sem.at[s])
                for k in range(TOP_K)]

    def start_tile(idx_ref, s):
        def body(g, carry):
            for u in range(DMA_UNROLL):
                for cp in row_copies(idx_ref, s, g * DMA_UNROLL + u):
                    cp.start()
            return carry
        lax.fori_loop(0, tm // DMA_UNROLL, body, 0)

    @pl.when(i == 0)
    def _():
        start_tile(dest_ref, 0)

    @pl.when(i + 1 < n_steps)
    def _():
        start_tile(dest_next_ref, 1 - slot)

    def wait(g, carry):
        for u in range(DMA_UNROLL):
            for cp in row_copies(dest_ref, slot, g * DMA_UNROLL + u):
                cp.wait()
        return carry
    lax.fori_loop(0, tm // DMA_UNROLL, wait, 0)

    info = info_ref[...]
    f = None
    for k in range(TOP_K):
        term = info[:, 4 + k:5 + k] * _unpack_rows(rows_ref[slot, k])
        f = term if f is None else f + term
    y = DN_ALPHA * x_ref[...] + f
    o_ref[...] = _layer_norm(y, g_ref[...], b_ref[...])


def _combine(dest, info, x, yb, ln_g, ln_b):
    t = x.shape[0]
    n_steps = t // ROW_TILE
    assert ROW_TILE % DMA_UNROLL == 0
    dest_blocks = dest.reshape(n_steps, 1, TOP_K * ROW_TILE)
    idx_block = (1, 1, TOP_K * ROW_TILE)
    return pl.pallas_call(
        _combine_kernel,
        grid=(n_steps,),
        in_specs=[
            pl.BlockSpec(idx_block, lambda i: (i, 0, 0), memory_space=pltpu.SMEM),
            pl.BlockSpec(idx_block, lambda i: (jnp.minimum(i + 1, n_steps - 1), 0, 0), memory_space=pltpu.SMEM),
            pl.BlockSpec((ROW_TILE, LANES), lambda i: (i, 0)),
            pl.BlockSpec((ROW_TILE, D_MODEL), lambda i: (i, 0)),
            pl.BlockSpec(memory_space=pl.ANY),
            _resident((1, D_MODEL)),
            _resident((1, D_MODEL)),
        ],
        out_specs=pl.BlockSpec((ROW_TILE, D_MODEL), lambda i: (i, 0)),
        out_shape=jax.ShapeDtypeStruct((t, D_MODEL), F32),
        scratch_shapes=[
            pltpu.VMEM((2, TOP_K, ROW_TILE, PACK_W), jnp.uint32),
            pltpu.SemaphoreType.DMA((2,)),
        ],
        compiler_params=_cparams(1),
        name="combine",
    )(dest_blocks, dest_blocks, info, x, yb, ln_g, ln_b)


def _moe(x, xp, wg1, bg1, wg2, bg2, layer, w1, w3, w2, ln_g, ln_b):
    t = x.shape[0]
    w_r = jnp.concatenate([wg1, jnp.transpose(wg2, (1, 0, 2)).reshape(D_MODEL, N_EXPERTS)], axis=1)
    w_r = jnp.pad(w_r, ((0, 0), (0, LANES - w_r.shape[1])))
    b_r = jnp.pad(jnp.concatenate([bg1, bg2.reshape(-1)]), (0, LANES - N_GROUPS - N_EXPERTS))[None, :]
    info, cnt = _router(x, w_r, b_r)

    expert = info[:, 0:2].astype(jnp.int32)
    rank = info[:, 2:4].astype(jnp.int32)
    counts = cnt[0, ROUTER_LANE0:ROUTER_LANE0 + N_EXPERTS].astype(jnp.int32)
    padded = ((counts + MOE_TILE - 1) // MOE_TILE) * MOE_TILE
    pend = jnp.cumsum(padded)
    pstart = pend - padded
    experts = jnp.arange(N_EXPERTS, dtype=jnp.int32)
    dest = jnp.sum(jnp.where(expert[:, :, None] == experts, pstart, 0), axis=-1) + rank
    n_blocks = (t * TOP_K) // MOE_TILE + N_EXPERTS
    blk_start = jnp.arange(n_blocks, dtype=jnp.int32) * MOE_TILE
    blk_e = jnp.minimum(jnp.sum(pend[None, :] <= blk_start[:, None], axis=1), N_EXPERTS - 1).astype(jnp.int32)
    n_used = (pend[-1:] // MOE_TILE).astype(jnp.int32)

    xbuf = _dispatch(counts, pstart.astype(jnp.int32), n_used, dest, xp, n_blocks * MOE_TILE)
    yb = _experts(xbuf, blk_e, n_used, layer, w1, w3, w2)
    return _combine(dest, info, x, yb, ln_g, ln_b)


def kernel(x, mem, a_w_in, a_sgu_ln_g, a_sgu_ln_b, a_ws, a_bs, a_w_out, b_w_in, b_lambda, b_subln_g, b_w_out, shared_w_kv, rel_bias, mem_w_kv, ln_g, ln_b, moe_wg1, moe_bg1, moe_wg2, moe_bg2, moe_w1, moe_w3, moe_w2):
    b_, s_, d_ = x.shape
    assert b_ == 1 and d_ == D_MODEL
    h = x.reshape(s_, d_)
    kt_mem, v_mem = _memkv(mem.reshape(MEM_TOKENS, d_), mem_w_kv)
    bias_table = _near_bias_table(rel_bias)
    shared_k = shared_v = None
    for l in range(DEPTH):
        if l < N_A_LAYERS:
            i = l
            mix, qm = _pre_a(h, a_w_in[i].astype(BF16), a_sgu_ln_g[i][None, :], a_sgu_ln_b[i][None, :],
                             a_ws[i], jnp.transpose(a_bs[i]))
            w_out = a_w_out[i]
        else:
            i = l - N_A_LAYERS
            qd, qm = _pre_b(h, jnp.transpose(b_w_in[i][:, :DIFF_QK_W]).astype(BF16),
                            b_w_in[i][:, DIFF_QK_W:].astype(BF16))
            lambda_init = 0.8 - 0.6 * math.exp(-0.3 * l)
            mix = _attention(qd, shared_k, shared_v, bias_table, b_lambda[i], b_subln_g[i][None, :],
                             lambda_init)
            w_out = b_w_out[i]
        h, hp = _post(mix, qm, kt_mem, v_mem, l, h, w_out.astype(BF16), ln_g[l, 0][None, :], ln_b[l, 0][None, :])
        h = _moe(h, hp, moe_wg1[l], moe_bg1[l], moe_wg2[l], moe_bg2[l], l, moe_w1, moe_w3, moe_w2,
                 ln_g[l, 1][None, :], ln_b[l, 1][None, :])
        if l == N_A_LAYERS - 1:
            shared_k, shared_v = _kvproj(h, shared_w_kv[:, :DIFF_QK_W].astype(BF16),
                                         jnp.transpose(shared_w_kv[:, DIFF_QK_W:]).astype(BF16))
    return h.reshape(b_, s_, d_)
```

```python
import functools
import math

import jax
import jax.numpy as jnp
from jax import lax
from jax.experimental import pallas as pl
from jax.experimental.pallas import tpu as pltpu

F32 = jnp.float32
BF16 = jnp.bfloat16

D_MODEL = 2048
DEPTH = 4
CHUNK = 64
N_A_LAYERS = DEPTH // 2
HEAD_DIM = 128
MEM_TOKENS = 256
MEM_HEADS = 4
MEM_W = MEM_HEADS * HEAD_DIM
MIX_W = D_MODEL - MEM_W
SGU_CHUNK = 128
SGU_GROUPS = 4
SGU_GROUP_W = MIX_W // SGU_GROUPS
DIFF_HEADS = MIX_W // (2 * HEAD_DIM)
DIFF_QK_W = 2 * DIFF_HEADS * HEAD_DIM
DIFF_V_DIM = 2 * HEAD_DIM
REL_BUCKETS = 32
REL_MAX_DIST = 128
N_GROUPS = 4
EXPERTS_PER_GROUP = 8
N_EXPERTS = N_GROUPS * EXPERTS_PER_GROUP
TOP_K = 2
EXPERT_FF = 512
DN_ALPHA = (2 * DEPTH) ** 0.25
LN_EPS = 1e-5
QK_SCALE = HEAD_DIM ** -0.5
SQRT_HALF = math.sqrt(0.5)
LOG2E = math.log2(math.e)
NEG_BIG = -1e30

LANES = 128
VMEM_LIMIT_BYTES = 56 * 1024 * 1024

ROW_TILE = 256
ROUTER_TILE = 512
ATT_TILE = 256
ATT_NEAR = 4
MOE_TILE = 128
ROUTER_LANE0 = N_GROUPS


def _cparams(n_axes=1):
    return pltpu.CompilerParams(
        dimension_semantics=("arbitrary",) * n_axes,
        vmem_limit_bytes=VMEM_LIMIT_BYTES,
    )


def _dot(a, b):
    return jnp.dot(a, b, preferred_element_type=F32)


def _dot_nt(a, b):
    return lax.dot_general(a, b, (((1,), (1,)), ((), ())), preferred_element_type=F32)


def _gelu(x):
    return 0.5 * x * (1.0 + lax.erf(x * SQRT_HALF))


def _layer_norm(y, g, b):
    mu = jnp.mean(y, axis=-1, keepdims=True)
    d = y - mu
    var = jnp.mean(d * d, axis=-1, keepdims=True)
    return d * lax.rsqrt(var + LN_EPS) * g + b


HI16 = 0xFFFF0000
PACK_W = D_MODEL // 2


def _pack_rows(y):
    lo = y[:, :PACK_W].astype(BF16).astype(F32)
    hi = y[:, PACK_W:].astype(BF16).astype(F32)
    return (pltpu.bitcast(lo, jnp.uint32) >> 16) | (pltpu.bitcast(hi, jnp.uint32) & jnp.uint32(HI16))


def _unpack_rows(w):
    lo = pltpu.bitcast(w << 16, F32)
    hi = pltpu.bitcast(w & jnp.uint32(HI16), F32)
    return jnp.concatenate([lo, hi], axis=-1)


def _resident(shape):
    nd = len(shape)
    return pl.BlockSpec(shape, lambda *_: (0,) * nd, pipeline_mode=pl.Buffered(1))


def _memkv_kernel(mem_ref, w_ref, kt_ref, v_ref):
    kv = _dot(mem_ref[...].astype(BF16), w_ref[0].astype(BF16))
    kt_ref[0] = kv[:, :MEM_W].T.astype(BF16)
    v_ref[0] = kv[:, MEM_W:].astype(BF16)


def _memkv(mem, mem_w_kv):
    n_layers = mem_w_kv.shape[0]
    m = mem.shape[0]
    return pl.pallas_call(
        _memkv_kernel,
        grid=(n_layers,),
        in_specs=[
            pl.BlockSpec((m, D_MODEL), lambda l: (0, 0)),
            pl.BlockSpec((1, D_MODEL, 2 * MEM_W), lambda l: (l, 0, 0)),
        ],
        out_specs=[
            pl.BlockSpec((1, MEM_W, m), lambda l: (l, 0, 0)),
            pl.BlockSpec((1, m, MEM_W), lambda l: (l, 0, 0)),
        ],
        out_shape=[
            jax.ShapeDtypeStruct((n_layers, MEM_W, m), BF16),
            jax.ShapeDtypeStruct((n_layers, m, MEM_W), BF16),
        ],
        compiler_params=_cparams(1),
        name="memkv",
    )(mem, mem_w_kv)


def _pre_a_kernel(x_ref, w_ref, lng_ref, lnb_ref, ws_ref, bst_ref, mix_ref, qm_ref, vn_ref):
    tm = x_ref.shape[0]
    x = x_ref[...].astype(BF16)
    v = _gelu(_dot(x, w_ref[:, MIX_W:2 * MIX_W]))
    vn_ref[...] = _layer_norm(v, lng_ref[...], lnb_ref[...]).astype(BF16)
    row = lax.broadcasted_iota(jnp.int32, (SGU_CHUNK, SGU_CHUNK), 0)
    col = lax.broadcasted_iota(jnp.int32, (SGU_CHUNK, SGU_CHUNK), 1)
    shift = CHUNK.bit_length() - 1
    keep = (col >> shift) <= (row >> shift)
    for g in range(SGU_GROUPS):
        cols = slice(g * SGU_GROUP_W, (g + 1) * SGU_GROUP_W)
        w_sp = jnp.where(keep, ws_ref[g], 0.0).astype(BF16)
        u = _gelu(_dot(x, w_ref[:, cols]))
        bias = bst_ref[:, g:g + 1]
        for c in range(tm // SGU_CHUNK):
            rows = slice(c * SGU_CHUNK, (c + 1) * SGU_CHUNK)
            gate = _dot(w_sp, vn_ref[rows, cols]) + bias
            mix_ref[rows, cols] = (u[rows] * gate).astype(BF16)
    qm_ref[...] = (_dot(x, w_ref[:, 2 * MIX_W:]) * QK_SCALE).astype(BF16)


def _pre_a(x, w_in, ln_g, ln_b, ws, bs_t):
    t = x.shape[0]
    n_in = w_in.shape[1]
    return pl.pallas_call(
        _pre_a_kernel,
        grid=(t // ROW_TILE,),
        in_specs=[
            pl.BlockSpec((ROW_TILE, D_MODEL), lambda i: (i, 0)),
            _resident((D_MODEL, n_in)),
            _resident((1, MIX_W)),
            _resident((1, MIX_W)),
            _resident((SGU_GROUPS, SGU_CHUNK, SGU_CHUNK)),
            _resident((SGU_CHUNK, SGU_GROUPS)),
        ],
        out_specs=[
            pl.BlockSpec((ROW_TILE, MIX_W), lambda i: (i, 0)),
            pl.BlockSpec((ROW_TILE, MEM_W), lambda i: (i, 0)),
        ],
        out_shape=[
            jax.ShapeDtypeStruct((t, MIX_W), BF16),
            jax.ShapeDtypeStruct((t, MEM_W), BF16),
        ],
        scratch_shapes=[pltpu.VMEM((ROW_TILE, MIX_W), BF16)],
        compiler_params=_cparams(1),
        name="pre_a",
    )(x, w_in, ln_g, ln_b, ws, bs_t)


def _pre_b_kernel(x_ref, wqt_ref, wm_ref, qt_ref, qm_ref):
    x = x_ref[...].astype(BF16)
    for h in range(DIFF_HEADS):
        rows = slice(h * 2 * HEAD_DIM, (h + 1) * 2 * HEAD_DIM)
        qt_ref[h, 0] = (_dot_nt(wqt_ref[rows, :], x) * (QK_SCALE * LOG2E)).astype(BF16)
    qm_ref[...] = (_dot(x, wm_ref[...]) * QK_SCALE).astype(BF16)


def _pre_b(x, w_q_t, w_mem):
    t = x.shape[0]
    return pl.pallas_call(
        _pre_b_kernel,
        grid=(t // ATT_TILE,),
        in_specs=[
            pl.BlockSpec((ATT_TILE, D_MODEL), lambda i: (i, 0)),
            _resident((DIFF_QK_W, D_MODEL)),
            _resident((D_MODEL, MEM_W)),
        ],
        out_specs=[
            pl.BlockSpec((DIFF_HEADS, 1, 2 * HEAD_DIM, ATT_TILE), lambda i: (0, i, 0, 0)),
            pl.BlockSpec((ATT_TILE, MEM_W), lambda i: (i, 0)),
        ],
        out_shape=[
            jax.ShapeDtypeStruct((DIFF_HEADS, t // ATT_TILE, 2 * HEAD_DIM, ATT_TILE), BF16),
            jax.ShapeDtypeStruct((t, MEM_W), BF16),
        ],
        compiler_params=_cparams(1),
        name="pre_b",
    )(x, w_q_t, w_mem)


def _kvproj_kernel(x_ref, wk_ref, wvt_ref, k_ref, vt_ref):
    x = x_ref[...].astype(BF16)
    k_ref[...] = _dot(x, wk_ref[...]).astype(BF16)
    for h in range(DIFF_HEADS):
        rows = slice(h * DIFF_V_DIM, (h + 1) * DIFF_V_DIM)
        vt_ref[h, 0] = _dot_nt(wvt_ref[rows, :], x).astype(BF16)


def _kvproj(x, w_k, w_v_t):
    t = x.shape[0]
    n_v = DIFF_HEADS * DIFF_V_DIM
    return pl.pallas_call(
        _kvproj_kernel,
        grid=(t // ATT_TILE,),
        in_specs=[
            pl.BlockSpec((ATT_TILE, D_MODEL), lambda i: (i, 0)),
            _resident((D_MODEL, DIFF_QK_W)),
            _resident((n_v, D_MODEL)),
        ],
        out_specs=[
            pl.BlockSpec((ATT_TILE, DIFF_QK_W), lambda i: (i, 0)),
            pl.BlockSpec((DIFF_HEADS, 1, DIFF_V_DIM, ATT_TILE), lambda i: (0, i, 0, 0)),
        ],
        out_shape=[
            jax.ShapeDtypeStruct((t, DIFF_QK_W), BF16),
            jax.ShapeDtypeStruct((DIFF_HEADS, t // ATT_TILE, DIFF_V_DIM, ATT_TILE), BF16),
        ],
        compiler_params=_cparams(1),
        name="kvproj",
    )(x, w_k, w_v_t)


def _attn_kernel(qt_ref, k_ref, vt_ref, bias_ref, lam_ref, g_ref, o_ref, acc1_ref, acc2_ref,
                 s_ref, p_ref, *, lambda_init):
    i = pl.program_id(1)
    tq = qt_ref.shape[3]
    acc1_ref[...] = jnp.zeros_like(acc1_ref)
    acc2_ref[...] = jnp.zeros_like(acc2_ref)

    def scores(j0, n_blk, slot):
        start = pl.multiple_of(j0 * tq, tq)
        kblk = k_ref[pl.ds(start, n_blk * tq), :]
        for s in range(2):
            rows = slice(s * HEAD_DIM, (s + 1) * HEAD_DIM)
            s_ref[slot, s, :n_blk * tq, :] = _dot(kblk[:, rows], qt_ref[0, 0, rows, :])

    def probs(n_blk, bias, stats, slot):
        new_stats, alphas = [], []
        for s in range(2):
            m, l = stats[2 * s], stats[2 * s + 1]
            st = s_ref[slot, s, :n_blk * tq, :]
            if bias is not None:
                st = st + bias
            m_new = jnp.maximum(m, jnp.max(st, axis=0, keepdims=True))
            alpha = jnp.exp2(m - m_new)
            p = jnp.exp2(st - m_new)
            p_ref[slot, s, :n_blk * tq, :] = p.astype(BF16)
            new_stats += [m_new, alpha * l + jnp.sum(p, axis=0, keepdims=True)]
            alphas.append(alpha)
        return tuple(new_stats), tuple(alphas)

    def accumulate(j0, n_blk, alphas, slot):
        vts = [vt_ref[0, j0 + n] for n in range(n_blk)]
        for s, acc_ref in enumerate((acc1_ref, acc2_ref)):
            pv = _dot(vts[0], p_ref[slot, s, :tq, :])
            for n in range(1, n_blk):
                pv = pv + _dot(vts[n], p_ref[slot, s, n * tq:(n + 1) * tq, :])
            acc_ref[...] = alphas[s] * acc_ref[...] + pv

    neg = jnp.full((1, tq), NEG_BIG, F32)
    zero = jnp.zeros((1, tq), F32)
    stats = (neg, zero, neg, zero)
    n_pairs = jnp.maximum(i + 1 - (ATT_NEAR - 1), 0) // 2
    n_near = i + 1 - 2 * n_pairs

    def far_pairs(stats):
        scores(0, 2, 0)

        def first_two(stats):
            out = probs(2, None, stats, 0)
            scores(2, 2, 1)
            return out
        stats, alphas = lax.cond(n_pairs >= 2, first_two, lambda c: probs(2, None, c, 0), stats)

        def body(t, carry):
            stats, alphas = carry
            accumulate(2 * (t - 2), 2, alphas, lax.rem(t, 2))
            out = probs(2, None, stats, lax.rem(t - 1, 2))
            scores(2 * t, 2, lax.rem(t, 2))
            return out
        carry = lax.fori_loop(2, n_pairs, body, (stats, alphas))

        def last_two(carry):
            stats, alphas = carry
            accumulate(2 * (n_pairs - 2), 2, alphas, lax.rem(n_pairs, 2))
            return probs(2, None, stats, lax.rem(n_pairs - 1, 2))
        stats, alphas = lax.cond(n_pairs >= 2, last_two, lambda c: c, carry)
        accumulate(2 * (n_pairs - 1), 2, alphas, lax.rem(n_pairs - 1, 2))
        return stats

    def near(n_blk):
        def run(stats):
            j0 = i + 1 - n_blk
            scores(j0, n_blk, 0)
            stats, alphas = probs(n_blk, bias_ref[0, (ATT_NEAR - n_blk) * tq:, :], stats, 0)
            accumulate(j0, n_blk, alphas, 0)
            return stats
        return run

    stats = lax.cond(n_pairs > 0, far_pairs, lambda c: c, stats)
    assert ATT_NEAR == 4
    m1, l1, m2, l2 = lax.cond(
        n_near <= 2,
        lambda c: lax.cond(n_near == 1, near(1), near(2), c),
        lambda c: lax.cond(n_near == 3, near(3), near(4), c),
        stats)

    lp = lam_ref[...]
    lam = (jnp.exp(jnp.sum(lp[0:1] * lp[1:2], axis=-1, keepdims=True))
           - jnp.exp(jnp.sum(lp[2:3] * lp[3:4], axis=-1, keepdims=True)) + lambda_init)
    ot = acc1_ref[...] / l1 - lam * (acc2_ref[...] / l2)
    ms = jnp.mean(ot * ot, axis=0, keepdims=True)
    ot = ot * lax.rsqrt(ms + LN_EPS)
    o_ref[...] = (ot.T * (g_ref[...] * (1.0 - lambda_init))).astype(o_ref.dtype)


def _attention(qt, k, vt, bias, lam_params, subln_g, lambda_init):
    t = k.shape[0]
    n_blk = t // ATT_TILE
    kern = functools.partial(_attn_kernel, lambda_init=lambda_init)
    return pl.pallas_call(
        kern,
        grid=(DIFF_HEADS, n_blk),
        in_specs=[
            pl.BlockSpec((1, 1, 2 * HEAD_DIM, ATT_TILE), lambda h, i: (h, i, 0, 0)),
            pl.BlockSpec((t, 2 * HEAD_DIM), lambda h, i: (0, h)),
            pl.BlockSpec((1, n_blk, DIFF_V_DIM, ATT_TILE), lambda h, i: (h, 0, 0, 0)),
            pl.BlockSpec((1, ATT_NEAR * ATT_TILE, ATT_TILE), lambda h, i: (h, 0, 0)),
            pl.BlockSpec((4, HEAD_DIM), lambda h, i: (0, 0)),
            pl.BlockSpec((1, DIFF_V_DIM), lambda h, i: (0, 0)),
        ],
        out_specs=pl.BlockSpec((ATT_TILE, DIFF_V_DIM), lambda h, i: (i, h)),
        out_shape=jax.ShapeDtypeStruct((t, DIFF_HEADS * DIFF_V_DIM), BF16),
        scratch_shapes=[pltpu.VMEM((DIFF_V_DIM, ATT_TILE), F32),
                        pltpu.VMEM((DIFF_V_DIM, ATT_TILE), F32),
                        pltpu.VMEM((2, 2, ATT_NEAR * ATT_TILE, ATT_TILE), F32),
                        pltpu.VMEM((2, 2, ATT_NEAR * ATT_TILE, ATT_TILE), BF16)],
        compiler_params=_cparams(2),
        name="diff_attn",
    )(qt, k, vt, bias, lam_params, subln_g)


def _relative_bucket(rel):
    n = REL_BUCKETS // 2
    max_exact = n // 2
    ret = jnp.where(rel > 0, n, 0)
    a = jnp.abs(rel)
    af = jnp.maximum(a, 1).astype(jnp.float32)
    large = max_exact + (jnp.log(af / max_exact) / math.log(REL_MAX_DIST / max_exact)
                         * (n - max_exact)).astype(jnp.int32)
    large = jnp.minimum(large, n - 1)
    return ret + jnp.where(a < max_exact, a, large)


def _near_bias_table(rel_bias):
    tq = ATT_TILE
    assert tq >= REL_MAX_DIST and tq % CHUNK == 0
    qpos = jnp.arange(tq)[:, None]
    kpos = jnp.arange(ATT_NEAR * tq)[None, :] - (ATT_NEAR - 1) * tq
    onehot = jax.nn.one_hot(_relative_bucket(kpos - qpos), REL_BUCKETS, dtype=F32)
    bias = jnp.einsum('qkb,bh->qkh', onehot, rel_bias.astype(F32),
                      precision=lax.Precision.HIGHEST)
    far = rel_bias[_relative_bucket(jnp.array(-ATT_NEAR * tq))].astype(F32)
    visible = (kpos // CHUNK) <= (qpos // CHUNK)
    table = jnp.where(visible[:, :, None], (bias - far) * LOG2E, NEG_BIG)
    return jnp.transpose(table, (2, 1, 0))


def _post_kernel(mix_ref, qm_ref, kt_ref, vm_ref, x_ref, w_ref, g_ref, b_ref, o_ref, op_ref):
    heads = []
    for h in range(MEM_HEADS):
        cols = slice(h * HEAD_DIM, (h + 1) * HEAD_DIM)
        s = _dot(qm_ref[:, cols], kt_ref[0, cols, :])
        p = jnp.exp(s - jnp.max(s, axis=-1, keepdims=True))
        l = jnp.sum(p, axis=-1, keepdims=True)
        heads.append((_dot(p.astype(BF16), vm_ref[0, :, cols]) / l).astype(BF16))
    mem_out = jnp.concatenate(heads, axis=-1)
    t = _dot(mix_ref[...], w_ref[:MIX_W, :]) + _dot(mem_out, w_ref[MIX_W:, :])
    y = _layer_norm(DN_ALPHA * x_ref[...] + t, g_ref[...], b_ref[...])
    o_ref[...] = y
    op_ref[...] = _pack_rows(y)


def _post(mix, qm, kt, vm, layer, x, w_out, ln_g, ln_b):
    t = x.shape[0]
    m = kt.shape[2]
    return pl.pallas_call(
        _post_kernel,
        grid=(t // ROW_TILE,),
        in_specs=[
            pl.BlockSpec((ROW_TILE, MIX_W), lambda i: (i, 0)),
            pl.BlockSpec((ROW_TILE, MEM_W), lambda i: (i, 0)),
            pl.BlockSpec((1, MEM_W, m), lambda i: (layer, 0, 0), pipeline_mode=pl.Buffered(1)),
            pl.BlockSpec((1, m, MEM_W), lambda i: (layer, 0, 0), pipeline_mode=pl.Buffered(1)),
            pl.BlockSpec((ROW_TILE, D_MODEL), lambda i: (i, 0)),
            _resident((D_MODEL, D_MODEL)),
            _resident((1, D_MODEL)),
            _resident((1, D_MODEL)),
        ],
        out_specs=[
            pl.BlockSpec((ROW_TILE, D_MODEL), lambda i: (i, 0)),
            pl.BlockSpec((ROW_TILE, PACK_W), lambda i: (i, 0)),
        ],
        out_shape=[
            jax.ShapeDtypeStruct((t, D_MODEL), F32),
            jax.ShapeDtypeStruct((t, PACK_W), jnp.uint32),
        ],
        compiler_params=_cparams(1),
        name="post",
    )(mix, qm, kt, vm, x, w_out, ln_g, ln_b)


def _router_kernel(x_ref, w_ref, b_ref, info_ref, cnt_ref, carry_ref):
    step = pl.program_id(0)
    tm = x_ref.shape[0]

    @pl.when(step == 0)
    def _():
        carry_ref[...] = jnp.zeros_like(carry_ref)

    logits = jnp.dot(x_ref[...], w_ref[...], precision=lax.Precision.HIGHEST,
                     preferred_element_type=F32) + b_ref[...]
    lane = lax.broadcasted_iota(jnp.int32, (tm, LANES), 1).astype(F32)
    no_lane = float(LANES)

    def top(mask):
        val = jnp.max(jnp.where(mask, logits, -jnp.inf), axis=-1, keepdims=True)
        idx = jnp.min(jnp.where(mask & (logits == val), lane, no_lane), axis=-1, keepdims=True)
        return val, idx

    is_group = lane < float(N_GROUPS)
    g_val, g_idx = top(is_group)
    p_sel = 1.0 / jnp.sum(jnp.where(is_group, jnp.exp(logits - g_val), 0.0), axis=-1, keepdims=True)
    first = float(ROUTER_LANE0) + float(EXPERTS_PER_GROUP) * g_idx
    in_group = (lane >= first) & (lane < first + float(EXPERTS_PER_GROUP))
    v1, i1 = top(in_group)
    v2, i2 = top(in_group & (lane != i1))
    e2 = jnp.exp(v2 - v1)
    gate1 = p_sel / (1.0 + e2)
    gate2 = p_sel * e2 / (1.0 + e2)

    hit1 = lane == i1
    hit2 = lane == i2
    onehot = jnp.where(hit1 | hit2, 1.0, 0.0)
    r = lax.broadcasted_iota(jnp.int32, (tm, tm), 0)
    c = lax.broadcasted_iota(jnp.int32, (tm, tm), 1)
    strict_lower = jnp.where(c < r, 1.0, 0.0).astype(BF16)
    before = _dot(strict_lower, onehot.astype(BF16)) + carry_ref[0:1, :]
    rank1 = jnp.sum(jnp.where(hit1, before, 0.0), axis=-1, keepdims=True)
    rank2 = jnp.sum(jnp.where(hit2, before, 0.0), axis=-1, keepdims=True)
    total = carry_ref[0:1, :] + jnp.sum(onehot, axis=0, keepdims=True)
    carry_ref[...] = jnp.broadcast_to(total, carry_ref.shape)
    cnt_ref[...] = jnp.broadcast_to(total, cnt_ref.shape)

    vals = (i1 - float(ROUTER_LANE0), i2 - float(ROUTER_LANE0), rank1, rank2, gate1, gate2)
    info = jnp.zeros((tm, LANES), F32)
    for k, val in enumerate(vals):
        info = jnp.where(lane == float(k), val, info)
    info_ref[...] = info


def _router(x, w_r, b_r):
    t = x.shape[0]
    return pl.pallas_call(
        _router_kernel,
        grid=(t // ROUTER_TILE,),
        in_specs=[
            pl.BlockSpec((ROUTER_TILE, D_MODEL), lambda i: (i, 0)),
            _resident((D_MODEL, LANES)),
            _resident((1, LANES)),
        ],
        out_specs=[
            pl.BlockSpec((ROUTER_TILE, LANES), lambda i: (i, 0)),
            pl.BlockSpec((8, LANES), lambda i: (0, 0)),
        ],
        out_shape=[
            jax.ShapeDtypeStruct((t, LANES), F32),
            jax.ShapeDtypeStruct((8, LANES), F32),
        ],
        scratch_shapes=[pltpu.VMEM((8, LANES), F32)],
        compiler_params=_cparams(1),
        name="router",
    )(x, w_r, b_r)


DISPATCH_TILE = 512
DMA_UNROLL = 8


def _dispatch_kernel(cnt_ref, pstart_ref, n_used_ref, dest_ref, xp_ref, xbuf_hbm, zero_ref, sem, zsem):
    step = pl.program_id(0)
    n_tok = dest_ref.shape[2] // TOP_K
    n_blocks = xbuf_hbm.shape[0] // MOE_TILE

    def token_copies(r):
        return [pltpu.make_async_copy(xp_ref.at[pl.ds(r, 1), :],
                                      xbuf_hbm.at[pl.ds(dest_ref[0, 0, TOP_K * r + k], 1), :], sem)
                for k in range(TOP_K)]

    def pad_rows(e):
        first = pstart_ref[e] + cnt_ref[e]
        n_pad = (-cnt_ref[e]) & (MOE_TILE - 1)
        return first, n_pad

    def zero_copy(row):
        return pltpu.make_async_copy(zero_ref.at[pl.ds(0, 1), :], xbuf_hbm.at[pl.ds(row, 1), :], zsem)

    def zero_block_copy(b):
        return pltpu.make_async_copy(zero_ref, xbuf_hbm.at[pl.ds(b * MOE_TILE, MOE_TILE), :], zsem)

    @pl.when(step == 0)
    def _():
        zero_ref[...] = jnp.zeros_like(zero_ref)

        def start_expert(e, carry):
            first, n_pad = pad_rows(e)
            lax.fori_loop(0, n_pad, lambda r, c: (zero_copy(first + r).start(), c)[1], 0)
            return carry
        lax.fori_loop(0, N_EXPERTS, start_expert, 0)
        lax.fori_loop(n_used_ref[0], n_blocks, lambda b, c: (zero_block_copy(b).start(), c)[1], 0)

    def start(g, carry):
        for u in range(DMA_UNROLL):
            for cp in token_copies(g * DMA_UNROLL + u):
                cp.start()
        return carry
    lax.fori_loop(0, n_tok // DMA_UNROLL, start, 0)

    def wait(g, carry):
        for u in range(DMA_UNROLL):
            for cp in token_copies(g * DMA_UNROLL + u):
                cp.wait()
        return carry
    lax.fori_loop(0, n_tok // DMA_UNROLL, wait, 0)

    @pl.when(step == 0)
    def _():
        def wait_expert(e, carry):
            first, n_pad = pad_rows(e)
            lax.fori_loop(0, n_pad, lambda r, c: (zero_copy(first + r).wait(), c)[1], 0)
            return carry
        lax.fori_loop(0, N_EXPERTS, wait_expert, 0)
        lax.fori_loop(n_used_ref[0], n_blocks, lambda b, c: (zero_block_copy(b).wait(), c)[1], 0)


def _dispatch(counts, pstart, n_used, dest, xp, n_rows):
    t = xp.shape[0]
    assert MOE_TILE & (MOE_TILE - 1) == 0 and t % DISPATCH_TILE == 0 and DISPATCH_TILE % DMA_UNROLL == 0
    dest_blocks = dest.reshape(t // DISPATCH_TILE, 1, TOP_K * DISPATCH_TILE)
    grid_spec = pltpu.PrefetchScalarGridSpec(
        num_scalar_prefetch=3,
        grid=(t // DISPATCH_TILE,),
        in_specs=[
            pl.BlockSpec((1, 1, TOP_K * DISPATCH_TILE), lambda i, c, p, n: (i, 0, 0), memory_space=pltpu.SMEM),
            pl.BlockSpec((DISPATCH_TILE, PACK_W), lambda i, c, p, n: (i, 0)),
        ],
        out_specs=pl.BlockSpec(memory_space=pl.ANY),
        scratch_shapes=[
            pltpu.VMEM((MOE_TILE, PACK_W), jnp.uint32),
            pltpu.SemaphoreType.DMA(()),
            pltpu.SemaphoreType.DMA(()),
        ],
    )
    return pl.pallas_call(
        _dispatch_kernel,
        grid_spec=grid_spec,
        out_shape=jax.ShapeDtypeStruct((n_rows, PACK_W), jnp.uint32),
        compiler_params=_cparams(1),
        name="dispatch",
    )(counts, pstart, n_used, dest_blocks, xp)


def _expert_kernel(blk_e_ref, n_used_ref, x_ref, w1_ref, w3_ref, w2_ref, y_ref,
                   w1b_ref, w3b_ref, w2b_ref):
    b = pl.program_id(0)

    @pl.when(b < n_used_ref[0])
    def _():
        new_expert = jnp.logical_or(b == 0, blk_e_ref[b] != blk_e_ref[jnp.maximum(b - 1, 0)])

        @pl.when(new_expert)
        def _():
            w1b_ref[...] = w1_ref[0, 0].astype(BF16)
            w3b_ref[...] = w3_ref[0, 0].astype(BF16)
            w2b_ref[...] = w2_ref[0, 0].astype(BF16)

        xb = _unpack_rows(x_ref[...]).astype(BF16)
        h1 = _dot(xb, w1b_ref[...])
        h3 = _dot(xb, w3b_ref[...])
        h = (h1 * jax.nn.sigmoid(h1)) * h3
        y_ref[...] = _pack_rows(_dot(h.astype(BF16), w2b_ref[...]))

    @pl.when(b >= n_used_ref[0])
    def _():
        y_ref[...] = jnp.zeros_like(y_ref)


def _experts(xbuf, blk_e, n_used, layer, w1, w3, w2):
    n_blocks = xbuf.shape[0] // MOE_TILE

    def x_index(b, be, nu):
        return (jnp.minimum(b, jnp.maximum(nu[0] - 1, 0)), 0)

    grid_spec = pltpu.PrefetchScalarGridSpec(
        num_scalar_prefetch=2,
        grid=(n_blocks,),
        in_specs=[
            pl.BlockSpec((MOE_TILE, PACK_W), x_index),
            pl.BlockSpec((1, 1, D_MODEL, EXPERT_FF), lambda b, be, nu: (layer, be[b], 0, 0)),
            pl.BlockSpec((1, 1, D_MODEL, EXPERT_FF), lambda b, be, nu: (layer, be[b], 0, 0)),
            pl.BlockSpec((1, 1, EXPERT_FF, D_MODEL), lambda b, be, nu: (layer, be[b], 0, 0)),
        ],
        out_specs=pl.BlockSpec((MOE_TILE, PACK_W), lambda b, be, nu: (b, 0)),
        scratch_shapes=[
            pltpu.VMEM((D_MODEL, EXPERT_FF), BF16),
            pltpu.VMEM((D_MODEL, EXPERT_FF), BF16),
            pltpu.VMEM((EXPERT_FF, D_MODEL), BF16),
        ],
    )
    return pl.pallas_call(
        _expert_kernel,
        grid_spec=grid_spec,
        out_shape=jax.ShapeDtypeStruct(xbuf.shape, jnp.uint32),
        compiler_params=_cparams(1),
        name="experts",
    )(blk_e, n_used, xbuf, w1, w3, w2)


def _combine_kernel(dest_ref, dest_next_ref, info_ref, x_ref, y_hbm, g_ref, b_ref, o_ref, rows_ref, sem):
    i = pl.program_id(0)
    n_steps = pl.num_programs(0)
    tm = x_ref.shape[0]
    slot = lax.rem(i, 2)

    def row_copies(idx_ref, s, r):
        return [pltpu.make_async_copy(y_hbm.at[pl.ds(idx_ref[0, 0, TOP_K * r + k], 1), :],
                                      rows_ref.at[s, k, pl.ds(r, 1), :], sem.at[s])
                for k in range(TOP_K)]

    def start_tile(idx_ref, s):
        def body(g, carry):
            for u in range(DMA_UNROLL):
                for cp in row_copies(idx_ref, s, g * DMA_UNROLL + u):
                    cp.start()
            return carry
        lax.fori_loop(0, tm // DMA_UNROLL, body, 0)

    @pl.when(i == 0)
    def _():
        start_tile(dest_ref, 0)

    @pl.when(i + 1 < n_steps)
    def _():
        start_tile(dest_next_ref, 1 - slot)

    def wait(g, carry):
        for u in range(DMA_UNROLL):
            for cp in row_copies(dest_ref, slot, g * DMA_UNROLL + u):
                cp.wait()
        return carry
    lax.fori_loop(0, tm // DMA_UNROLL, wait, 0)

    info = info_ref[...]
    f = None
    for k in range(TOP_K):
        term = info[:, 4 + k:5 + k] * _unpack_rows(rows_ref[slot, k])
        f = term if f is None else f + term
    y = DN_ALPHA * x_ref[...] + f
    o_ref[...] = _layer_norm(y, g_ref[...], b_ref[...])


def _combine(dest, info, x, yb, ln_g, ln_b):
    t = x.shape[0]
    n_steps = t // ROW_TILE
    assert ROW_TILE % DMA_UNROLL == 0
    dest_blocks = dest.reshape(n_steps, 1, TOP_K * ROW_TILE)
    idx_block = (1, 1, TOP_K * ROW_TILE)
    return pl.pallas_call(
        _combine_kernel,
        grid=(n_steps,),
        in_specs=[
            pl.BlockSpec(idx_block, lambda i: (i, 0, 0), memory_space=pltpu.SMEM),
            pl.BlockSpec(idx_block, lambda i: (jnp.minimum(i + 1, n_steps - 1), 0, 0), memory_space=pltpu.SMEM),
            pl.BlockSpec((ROW_TILE, LANES), lambda i: (i, 0)),
            pl.BlockSpec((ROW_TILE, D_MODEL), lambda i: (i, 0)),
            pl.BlockSpec(memory_space=pl.ANY),
            _resident((1, D_MODEL)),
            _resident((1, D_MODEL)),
        ],
        out_specs=pl.BlockSpec((ROW_TILE, D_MODEL), lambda i: (i, 0)),
        out_shape=jax.ShapeDtypeStruct((t, D_MODEL), F32),
        scratch_shapes=[
            pltpu.VMEM((2, TOP_K, ROW_TILE, PACK_W), jnp.uint32),
            pltpu.SemaphoreType.DMA((2,)),
        ],
        compiler_params=_cparams(1),
        name="combine",
    )(dest_blocks, dest_blocks, info, x, yb, ln_g, ln_b)


def _moe(x, xp, wg1, bg1, wg2, bg2, layer, w1, w3, w2, ln_g, ln_b):
    t = x.shape[0]
    w_r = jnp.concatenate([wg1, jnp.transpose(wg2, (1, 0, 2)).reshape(D_MODEL, N_EXPERTS)], axis=1)
    w_r = jnp.pad(w_r, ((0, 0), (0, LANES - w_r.shape[1])))
    b_r = jnp.pad(jnp.concatenate([bg1, bg2.reshape(-1)]), (0, LANES - N_GROUPS - N_EXPERTS))[None, :]
    info, cnt = _router(x, w_r, b_r)

    expert = info[:, 0:2].astype(jnp.int32)
    rank = info[:, 2:4].astype(jnp.int32)
    counts = cnt[0, ROUTER_LANE0:ROUTER_LANE0 + N_EXPERTS].astype(jnp.int32)
    padded = ((counts + MOE_TILE - 1) // MOE_TILE) * MOE_TILE
    pend = jnp.cumsum(padded)
    pstart = pend - padded
    experts = jnp.arange(N_EXPERTS, dtype=jnp.int32)
    dest = jnp.sum(jnp.where(expert[:, :, None] == experts, pstart, 0), axis=-1) + rank
    n_blocks = (t * TOP_K) // MOE_TILE + N_EXPERTS
    blk_start = jnp.arange(n_blocks, dtype=jnp.int32) * MOE_TILE
    blk_e = jnp.minimum(jnp.sum(pend[None, :] <= blk_start[:, None], axis=1), N_EXPERTS - 1).astype(jnp.int32)
    n_used = (pend[-1:] // MOE_TILE).astype(jnp.int32)

    xbuf = _dispatch(counts, pstart.astype(jnp.int32), n_used, dest, xp, n_blocks * MOE_TILE)
    yb = _experts(xbuf, blk_e, n_used, layer, w1, w3, w2)
    return _combine(dest, info, x, yb, ln_g, ln_b)


def kernel(x, mem, a_w_in, a_sgu_ln_g, a_sgu_ln_b, a_ws, a_bs, a_w_out, b_w_in, b_lambda, b_subln_g, b_w_out, shared_w_kv, rel_bias, mem_w_kv, ln_g, ln_b, moe_wg1, moe_bg1, moe_wg2, moe_bg2, moe_w1, moe_w3, moe_w2):
    b_, s_, d_ = x.shape
    assert b_ == 1 and d_ == D_MODEL
    h = x.reshape(s_, d_)
    kt_mem, v_mem = _memkv(mem.reshape(MEM_TOKENS, d_), mem_w_kv)
    bias_table = _near_bias_table(rel_bias)
    shared_k = shared_v = None
    for l in range(DEPTH):
        if l < N_A_LAYERS:
            i = l
            mix, qm = _pre_a(h, a_w_in[i].astype(BF16), a_sgu_ln_g[i][None, :], a_sgu_ln_b[i][None, :],
                             a_ws[i], jnp.transpose(a_bs[i]))
            w_out = a_w_out[i]
        else:
            i = l - N_A_LAYERS
            qd, qm = _pre_b(h, jnp.transpose(b_w_in[i][:, :DIFF_QK_W]).astype(BF16),
                            b_w_in[i][:, DIFF_QK_W:].astype(BF16))
            lambda_init = 0.8 - 0.6 * math.exp(-0.3 * l)
            mix = _attention(qd, shared_k, shared_v, bias_table, b_lambda[i], b_subln_g[i][None, :],
                             lambda_init)
            w_out = b_w_out[i]
        h, hp = _post(mix, qm, kt_mem, v_mem, l, h, w_out.astype(BF16), ln_g[l, 0][None, :], ln_b[l, 0][None, :])
        h = _moe(h, hp, moe_wg1[l], moe_bg1[l], moe_wg2[l], moe_bg2[l], l, moe_w1, moe_w3, moe_w2,
                 ln_g[l, 1][None, :], ln_b[l, 1][None, :])
        if l == N_A_LAYERS - 1:
            shared_k, shared_v = _kvproj(h, shared_w_kv[:, :DIFF_QK_W].astype(BF16),
                                         jnp.transpose(shared_w_kv[:, DIFF_QK_W:]).astype(BF16))
    return h.reshape(b_, s_, d_)
```

```python
import functools
import math

import jax
import jax.numpy as jnp
from jax import lax
from jax.experimental import pallas as pl
from jax.experimental.pallas import tpu as pltpu

F32 = jnp.float32
BF16 = jnp.bfloat16

D_MODEL = 2048
DEPTH = 4
CHUNK = 64
N_A_LAYERS = DEPTH // 2
HEAD_DIM = 128
MEM_TOKENS = 256
MEM_HEADS = 4
MEM_W = MEM_HEADS * HEAD_DIM
MIX_W = D_MODEL - MEM_W
SGU_CHUNK = 128
SGU_GROUPS = 4
SGU_GROUP_W = MIX_W // SGU_GROUPS
DIFF_HEADS = MIX_W // (2 * HEAD_DIM)
DIFF_QK_W = 2 * DIFF_HEADS * HEAD_DIM
DIFF_V_DIM = 2 * HEAD_DIM
REL_BUCKETS = 32
REL_MAX_DIST = 128
N_GROUPS = 4
EXPERTS_PER_GROUP = 8
N_EXPERTS = N_GROUPS * EXPERTS_PER_GROUP
TOP_K = 2
EXPERT_FF = 512
DN_ALPHA = (2 * DEPTH) ** 0.25
LN_EPS = 1e-5
QK_SCALE = HEAD_DIM ** -0.5
SQRT_HALF = math.sqrt(0.5)
LOG2E = math.log2(math.e)
NEG_BIG = -1e30

LANES = 128
VMEM_LIMIT_BYTES = 56 * 1024 * 1024

ROW_TILE = 256
ROUTER_TILE = 512
ATT_TILE = 256
ATT_FAR = 2
ATT_NEAR = ATT_FAR + 1
MOE_TILE = 128
ROUTER_LANE0 = N_GROUPS


def _cparams(n_axes=1):
    return pltpu.CompilerParams(
        dimension_semantics=("arbitrary",) * n_axes,
        vmem_limit_bytes=VMEM_LIMIT_BYTES,
    )


def _dot(a, b):
    return jnp.dot(a, b, preferred_element_type=F32)


def _dot_nt(a, b):
    return lax.dot_general(a, b, (((1,), (1,)), ((), ())), preferred_element_type=F32)


def _gelu(x):
    return 0.5 * x * (1.0 + lax.erf(x * SQRT_HALF))


def _layer_norm(y, g, b):
    mu = jnp.mean(y, axis=-1, keepdims=True)
    d = y - mu
    var = jnp.mean(d * d, axis=-1, keepdims=True)
    return d * lax.rsqrt(var + LN_EPS) * g + b


HI16 = 0xFFFF0000
PACK_W = D_MODEL // 2


def _pack_rows(y):
    lo = y[:, :PACK_W].astype(BF16).astype(F32)
    hi = y[:, PACK_W:].astype(BF16).astype(F32)
    return (pltpu.bitcast(lo, jnp.uint32) >> 16) | (pltpu.bitcast(hi, jnp.uint32) & jnp.uint32(HI16))


def _unpack_rows(w):
    lo = pltpu.bitcast(w << 16, F32)
    hi = pltpu.bitcast(w & jnp.uint32(HI16), F32)
    return jnp.concatenate([lo, hi], axis=-1)


def _resident(shape):
    nd = len(shape)
    return pl.BlockSpec(shape, lambda *_: (0,) * nd, pipeline_mode=pl.Buffered(1))


def _memkv_kernel(mem_ref, w_ref, kt_ref, v_ref):
    kv = _dot(mem_ref[...].astype(BF16), w_ref[0].astype(BF16))
    kt_ref[0] = kv[:, :MEM_W].T.astype(BF16)
    v_ref[0] = kv[:, MEM_W:].astype(BF16)


def _memkv(mem, mem_w_kv):
    n_layers = mem_w_kv.shape[0]
    m = mem.shape[0]
    return pl.pallas_call(
        _memkv_kernel,
        grid=(n_layers,),
        in_specs=[
            pl.BlockSpec((m, D_MODEL), lambda l: (0, 0)),
            pl.BlockSpec((1, D_MODEL, 2 * MEM_W), lambda l: (l, 0, 0)),
        ],
        out_specs=[
            pl.BlockSpec((1, MEM_W, m), lambda l: (l, 0, 0)),
            pl.BlockSpec((1, m, MEM_W), lambda l: (l, 0, 0)),
        ],
        out_shape=[
            jax.ShapeDtypeStruct((n_layers, MEM_W, m), BF16),
            jax.ShapeDtypeStruct((n_layers, m, MEM_W), BF16),
        ],
        compiler_params=_cparams(1),
        name="memkv",
    )(mem, mem_w_kv)


def _pre_a_kernel(x_ref, w_ref, lng_ref, lnb_ref, ws_ref, bst_ref, mix_ref, qm_ref, vn_ref):
    tm = x_ref.shape[0]
    x = x_ref[...].astype(BF16)
    v = _gelu(_dot(x, w_ref[:, MIX_W:2 * MIX_W]))
    vn_ref[...] = _layer_norm(v, lng_ref[...], lnb_ref[...]).astype(BF16)
    row = lax.broadcasted_iota(jnp.int32, (SGU_CHUNK, SGU_CHUNK), 0)
    col = lax.broadcasted_iota(jnp.int32, (SGU_CHUNK, SGU_CHUNK), 1)
    shift = CHUNK.bit_length() - 1
    keep = (col >> shift) <= (row >> shift)
    for g in range(SGU_GROUPS):
        cols = slice(g * SGU_GROUP_W, (g + 1) * SGU_GROUP_W)
        w_sp = jnp.where(keep, ws_ref[g], 0.0).astype(BF16)
        u = _gelu(_dot(x, w_ref[:, cols]))
        bias = bst_ref[:, g:g + 1]
        for c in range(tm // SGU_CHUNK):
            rows = slice(c * SGU_CHUNK, (c + 1) * SGU_CHUNK)
            gate = _dot(w_sp, vn_ref[rows, cols]) + bias
            mix_ref[rows, cols] = (u[rows] * gate).astype(BF16)
    qm_ref[...] = (_dot(x, w_ref[:, 2 * MIX_W:]) * QK_SCALE).astype(BF16)


def _pre_a(x, w_in, ln_g, ln_b, ws, bs_t):
    t = x.shape[0]
    n_in = w_in.shape[1]
    return pl.pallas_call(
        _pre_a_kernel,
        grid=(t // ROW_TILE,),
        in_specs=[
            pl.BlockSpec((ROW_TILE, D_MODEL), lambda i: (i, 0)),
            _resident((D_MODEL, n_in)),
            _resident((1, MIX_W)),
            _resident((1, MIX_W)),
            _resident((SGU_GROUPS, SGU_CHUNK, SGU_CHUNK)),
            _resident((SGU_CHUNK, SGU_GROUPS)),
        ],
        out_specs=[
            pl.BlockSpec((ROW_TILE, MIX_W), lambda i: (i, 0)),
            pl.BlockSpec((ROW_TILE, MEM_W), lambda i: (i, 0)),
        ],
        out_shape=[
            jax.ShapeDtypeStruct((t, MIX_W), BF16),
            jax.ShapeDtypeStruct((t, MEM_W), BF16),
        ],
        scratch_shapes=[pltpu.VMEM((ROW_TILE, MIX_W), BF16)],
        compiler_params=_cparams(1),
        name="pre_a",
    )(x, w_in, ln_g, ln_b, ws, bs_t)


def _pre_b_kernel(x_ref, wqt_ref, wm_ref, qt_ref, qm_ref):
    x = x_ref[...].astype(BF16)
    for h in range(DIFF_HEADS):
        rows = slice(h * 2 * HEAD_DIM, (h + 1) * 2 * HEAD_DIM)
        qt_ref[h, 0] = (_dot_nt(wqt_ref[rows, :], x) * (QK_SCALE * LOG2E)).astype(BF16)
    qm_ref[...] = (_dot(x, wm_ref[...]) * QK_SCALE).astype(BF16)


def _pre_b(x, w_q_t, w_mem):
    t = x.shape[0]
    return pl.pallas_call(
        _pre_b_kernel,
        grid=(t // ATT_TILE,),
        in_specs=[
            pl.BlockSpec((ATT_TILE, D_MODEL), lambda i: (i, 0)),
            _resident((DIFF_QK_W, D_MODEL)),
            _resident((D_MODEL, MEM_W)),
        ],
        out_specs=[
            pl.BlockSpec((DIFF_HEADS, 1, 2 * HEAD_DIM, ATT_TILE), lambda i: (0, i, 0, 0)),
            pl.BlockSpec((ATT_TILE, MEM_W), lambda i: (i, 0)),
        ],
        out_shape=[
            jax.ShapeDtypeStruct((DIFF_HEADS, t // ATT_TILE, 2 * HEAD_DIM, ATT_TILE), BF16),
            jax.ShapeDtypeStruct((t, MEM_W), BF16),
        ],
        compiler_params=_cparams(1),
        name="pre_b",
    )(x, w_q_t, w_mem)


def _kvproj_kernel(x_ref, wk_ref, wvt_ref, k_ref, vt_ref):
    x = x_ref[...].astype(BF16)
    k_ref[...] = _dot(x, wk_ref[...]).astype(BF16)
    for h in range(DIFF_HEADS):
        rows = slice(h * DIFF_V_DIM, (h + 1) * DIFF_V_DIM)
        vt_ref[h, 0] = _dot_nt(wvt_ref[rows, :], x).astype(BF16)


def _kvproj(x, w_k, w_v_t):
    t = x.shape[0]
    n_v = DIFF_HEADS * DIFF_V_DIM
    return pl.pallas_call(
        _kvproj_kernel,
        grid=(t // ATT_TILE,),
        in_specs=[
            pl.BlockSpec((ATT_TILE, D_MODEL), lambda i: (i, 0)),
            _resident((D_MODEL, DIFF_QK_W)),
            _resident((n_v, D_MODEL)),
        ],
        out_specs=[
            pl.BlockSpec((ATT_TILE, DIFF_QK_W), lambda i: (i, 0)),
            pl.BlockSpec((DIFF_HEADS, 1, DIFF_V_DIM, ATT_TILE), lambda i: (0, i, 0, 0)),
        ],
        out_shape=[
            jax.ShapeDtypeStruct((t, DIFF_QK_W), BF16),
            jax.ShapeDtypeStruct((DIFF_HEADS, t // ATT_TILE, DIFF_V_DIM, ATT_TILE), BF16),
        ],
        compiler_params=_cparams(1),
        name="kvproj",
    )(x, w_k, w_v_t)


def _attn_kernel(qt_ref, k_ref, vt_ref, bias_ref, lam_ref, g_ref, o_ref, acc1_ref, acc2_ref,
                 s_ref, p_ref, *, lambda_init):
    i = pl.program_id(1)
    tq = qt_ref.shape[3]
    acc1_ref[...] = jnp.zeros_like(acc1_ref)
    acc2_ref[...] = jnp.zeros_like(acc2_ref)

    def scores(j0, n_blk, slot):
        start = pl.multiple_of(j0 * tq, tq)
        kblk = k_ref[pl.ds(start, n_blk * tq), :]
        for s in range(2):
            rows = slice(s * HEAD_DIM, (s + 1) * HEAD_DIM)
            s_ref[slot, s, :n_blk * tq, :] = _dot(kblk[:, rows], qt_ref[0, 0, rows, :])

    def probs(n_blk, bias, stats, slot):
        new_stats, alphas = [], []
        for s in range(2):
            m, l = stats[2 * s], stats[2 * s + 1]
            st = s_ref[slot, s, :n_blk * tq, :]
            if bias is not None:
                st = st + bias
            m_new = jnp.maximum(m, jnp.max(st, axis=0, keepdims=True))
            alpha = jnp.exp2(m - m_new)
            p = jnp.exp2(st - m_new)
            p_ref[slot, s, :n_blk * tq, :] = p.astype(BF16)
            new_stats += [m_new, alpha * l + jnp.sum(p, axis=0, keepdims=True)]
            alphas.append(alpha)
        return tuple(new_stats), tuple(alphas)

    def accumulate(j0, n_blk, alphas, slot):
        vts = [vt_ref[0, j0 + n] for n in range(n_blk)]
        for s, acc_ref in enumerate((acc1_ref, acc2_ref)):
            pv = _dot(vts[0], p_ref[slot, s, :tq, :])
            for n in range(1, n_blk):
                pv = pv + _dot(vts[n], p_ref[slot, s, n * tq:(n + 1) * tq, :])
            acc_ref[...] = alphas[s] * acc_ref[...] + pv

    neg = jnp.full((1, tq), NEG_BIG, F32)
    zero = jnp.zeros((1, tq), F32)
    stats = (neg, zero, neg, zero)
    fb = ATT_FAR
    n_far = jnp.maximum(i - 1, 0) // fb
    n_near = i + 1 - fb * n_far

    def far_steps(stats):
        scores(0, fb, 0)

        def first_two(stats):
            out = probs(fb, None, stats, 0)
            scores(fb, fb, 1)
            return out
        stats, alphas = lax.cond(n_far >= 2, first_two, lambda c: probs(fb, None, c, 0), stats)

        def steady(t, carry, parity):
            stats, alphas = carry
            accumulate(fb * (t - 2), fb, alphas, parity)
            out = probs(fb, None, stats, 1 - parity)
            scores(fb * t, fb, parity)
            return out

        def body(u, carry):
            t = 2 + 2 * u
            return steady(t + 1, steady(t, carry, 0), 1)
        n_steady = jnp.maximum(n_far - 2, 0)
        carry = lax.fori_loop(0, n_steady // 2, body, (stats, alphas))
        carry = lax.cond(lax.rem(n_steady, 2) == 1,
                         lambda c: steady(n_far - 1, c, 0), lambda c: c, carry)

        def last_two(carry):
            stats, alphas = carry
            accumulate(fb * (n_far - 2), fb, alphas, lax.rem(n_far, 2))
            return probs(fb, None, stats, lax.rem(n_far - 1, 2))
        stats, alphas = lax.cond(n_far >= 2, last_two, lambda c: c, carry)
        accumulate(fb * (n_far - 1), fb, alphas, lax.rem(n_far - 1, 2))
        return stats

    def near(n_blk):
        def run(stats):
            j0 = i + 1 - n_blk
            scores(j0, n_blk, 0)
            stats, alphas = probs(n_blk, bias_ref[0, (ATT_NEAR - n_blk) * tq:, :], stats, 0)
            accumulate(j0, n_blk, alphas, 0)
            return stats
        return run

    def near_from(n_blk):
        if n_blk == ATT_NEAR:
            return near(n_blk)
        return lambda c: lax.cond(n_near == n_blk, near(n_blk), near_from(n_blk + 1), c)

    stats = lax.cond(n_far > 0, far_steps, lambda c: c, stats)
    m1, l1, m2, l2 = near_from(1)(stats)

    lp = lam_ref[...]
    lam = (jnp.exp(jnp.sum(lp[0:1] * lp[1:2], axis=-1, keepdims=True))
           - jnp.exp(jnp.sum(lp[2:3] * lp[3:4], axis=-1, keepdims=True)) + lambda_init)
    ot = acc1_ref[...] / l1 - lam * (acc2_ref[...] / l2)
    ms = jnp.mean(ot * ot, axis=0, keepdims=True)
    ot = ot * lax.rsqrt(ms + LN_EPS)
    o_ref[...] = (ot.T * (g_ref[...] * (1.0 - lambda_init))).astype(o_ref.dtype)


def _attention(qt, k, vt, bias, lam_params, subln_g, lambda_init):
    t = k.shape[0]
    n_blk = t // ATT_TILE
    kern = functools.partial(_attn_kernel, lambda_init=lambda_init)
    return pl.pallas_call(
        kern,
        grid=(DIFF_HEADS, n_blk),
        in_specs=[
            pl.BlockSpec((1, 1, 2 * HEAD_DIM, ATT_TILE), lambda h, i: (h, i, 0, 0)),
            pl.BlockSpec((t, 2 * HEAD_DIM), lambda h, i: (0, h)),
            pl.BlockSpec((1, n_blk, DIFF_V_DIM, ATT_TILE), lambda h, i: (h, 0, 0, 0)),
            pl.BlockSpec((1, ATT_NEAR * ATT_TILE, ATT_TILE), lambda h, i: (h, 0, 0)),
            pl.BlockSpec((4, HEAD_DIM), lambda h, i: (0, 0)),
            pl.BlockSpec((1, DIFF_V_DIM), lambda h, i: (0, 0)),
        ],
        out_specs=pl.BlockSpec((ATT_TILE, DIFF_V_DIM), lambda h, i: (i, h)),
        out_shape=jax.ShapeDtypeStruct((t, DIFF_HEADS * DIFF_V_DIM), BF16),
        scratch_shapes=[pltpu.VMEM((DIFF_V_DIM, ATT_TILE), F32),
                        pltpu.VMEM((DIFF_V_DIM, ATT_TILE), F32),
                        pltpu.VMEM((2, 2, ATT_NEAR * ATT_TILE, ATT_TILE), F32),
                        pltpu.VMEM((2, 2, ATT_NEAR * ATT_TILE, ATT_TILE), BF16)],
        compiler_params=_cparams(2),
        name="diff_attn",
    )(qt, k, vt, bias, lam_params, subln_g)


def _relative_bucket(rel):
    n = REL_BUCKETS // 2
    max_exact = n // 2
    ret = jnp.where(rel > 0, n, 0)
    a = jnp.abs(rel)
    af = jnp.maximum(a, 1).astype(jnp.float32)
    large = max_exact + (jnp.log(af / max_exact) / math.log(REL_MAX_DIST / max_exact)
                         * (n - max_exact)).astype(jnp.int32)
    large = jnp.minimum(large, n - 1)
    return ret + jnp.where(a < max_exact, a, large)


def _near_bias_table(rel_bias):
    tq = ATT_TILE
    assert tq >= REL_MAX_DIST and tq % CHUNK == 0
    qpos = jnp.arange(tq)[:, None]
    kpos = jnp.arange(ATT_NEAR * tq)[None, :] - (ATT_NEAR - 1) * tq
    onehot = jax.nn.one_hot(_relative_bucket(kpos - qpos), REL_BUCKETS, dtype=F32)
    bias = jnp.einsum('qkb,bh->qkh', onehot, rel_bias.astype(F32),
                      precision=lax.Precision.HIGHEST)
    far = rel_bias[_relative_bucket(jnp.array(-ATT_NEAR * tq))].astype(F32)
    visible = (kpos // CHUNK) <= (qpos // CHUNK)
    table = jnp.where(visible[:, :, None], (bias - far) * LOG2E, NEG_BIG)
    return jnp.transpose(table, (2, 1, 0))


def _post_kernel(mix_ref, qm_ref, kt_ref, vm_ref, x_ref, w_ref, g_ref, b_ref, o_ref, op_ref):
    heads = []
    for h in range(MEM_HEADS):
        cols = slice(h * HEAD_DIM, (h + 1) * HEAD_DIM)
        s = _dot(qm_ref[:, cols], kt_ref[0, cols, :])
        p = jnp.exp(s - jnp.max(s, axis=-1, keepdims=True))
        l = jnp.sum(p, axis=-1, keepdims=True)
        heads.append((_dot(p.astype(BF16), vm_ref[0, :, cols]) / l).astype(BF16))
    mem_out = jnp.concatenate(heads, axis=-1)
    t = _dot(mix_ref[...], w_ref[:MIX_W, :]) + _dot(mem_out, w_ref[MIX_W:, :])
    y = _layer_norm(DN_ALPHA * x_ref[...] + t, g_ref[...], b_ref[...])
    o_ref[...] = y
    op_ref[...] = _pack_rows(y)


def _post(mix, qm, kt, vm, layer, x, w_out, ln_g, ln_b):
    t = x.shape[0]
    m = kt.shape[2]
    return pl.pallas_call(
        _post_kernel,
        grid=(t // ROW_TILE,),
        in_specs=[
            pl.BlockSpec((ROW_TILE, MIX_W), lambda i: (i, 0)),
            pl.BlockSpec((ROW_TILE, MEM_W), lambda i: (i, 0)),
            pl.BlockSpec((1, MEM_W, m), lambda i: (layer, 0, 0), pipeline_mode=pl.Buffered(1)),
            pl.BlockSpec((1, m, MEM_W), lambda i: (layer, 0, 0), pipeline_mode=pl.Buffered(1)),
            pl.BlockSpec((ROW_TILE, D_MODEL), lambda i: (i, 0)),
            _resident((D_MODEL, D_MODEL)),
            _resident((1, D_MODEL)),
            _resident((1, D_MODEL)),
        ],
        out_specs=[
            pl.BlockSpec((ROW_TILE, D_MODEL), lambda i: (i, 0)),
            pl.BlockSpec((ROW_TILE, PACK_W), lambda i: (i, 0)),
        ],
        out_shape=[
            jax.ShapeDtypeStruct((t, D_MODEL), F32),
            jax.ShapeDtypeStruct((t, PACK_W), jnp.uint32),
        ],
        compiler_params=_cparams(1),
        name="post",
    )(mix, qm, kt, vm, x, w_out, ln_g, ln_b)


def _router_kernel(x_ref, w_ref, b_ref, info_ref, cnt_ref, carry_ref, wsplit_ref):
    step = pl.program_id(0)
    tm = x_ref.shape[0]

    @pl.when(step == 0)
    def _():
        carry_ref[...] = jnp.zeros_like(carry_ref)
        w = w_ref[...]
        wh = w.astype(BF16)
        wsplit_ref[:, :LANES] = wh
        wsplit_ref[:, LANES:] = (w - wh.astype(F32)).astype(BF16)

    x = x_ref[...]
    xh = x.astype(BF16)
    xl = (x - xh.astype(F32)).astype(BF16)
    both = _dot(xh, wsplit_ref[...])
    logits = both[:, :LANES] + (both[:, LANES:] + _dot(xl, wsplit_ref[:, :LANES])) + b_ref[...]
    lane = lax.broadcasted_iota(jnp.int32, (tm, LANES), 1).astype(F32)
    no_lane = float(LANES)

    def top(mask):
        val = jnp.max(jnp.where(mask, logits, -jnp.inf), axis=-1, keepdims=True)
        idx = jnp.min(jnp.where(mask & (logits == val), lane, no_lane), axis=-1, keepdims=True)
        return val, idx

    is_group = lane < float(N_GROUPS)
    g_val, g_idx = top(is_group)
    p_sel = 1.0 / jnp.sum(jnp.where(is_group, jnp.exp(logits - g_val), 0.0), axis=-1, keepdims=True)
    first = float(ROUTER_LANE0) + float(EXPERTS_PER_GROUP) * g_idx
    in_group = (lane >= first) & (lane < first + float(EXPERTS_PER_GROUP))
    v1, i1 = top(in_group)
    v2, i2 = top(in_group & (lane != i1))
    e2 = jnp.exp(v2 - v1)
    gate1 = p_sel / (1.0 + e2)
    gate2 = p_sel * e2 / (1.0 + e2)

    hit1 = lane == i1
    hit2 = lane == i2
    onehot = jnp.where(hit1 | hit2, 1.0, 0.0)
    r = lax.broadcasted_iota(jnp.int32, (tm, tm), 0)
    c = lax.broadcasted_iota(jnp.int32, (tm, tm), 1)
    strict_lower = jnp.where(c < r, 1.0, 0.0).astype(BF16)
    before = _dot(strict_lower, onehot.astype(BF16)) + carry_ref[0:1, :]
    rank1 = jnp.sum(jnp.where(hit1, before, 0.0), axis=-1, keepdims=True)
    rank2 = jnp.sum(jnp.where(hit2, before, 0.0), axis=-1, keepdims=True)
    total = carry_ref[0:1, :] + jnp.sum(onehot, axis=0, keepdims=True)
    carry_ref[...] = jnp.broadcast_to(total, carry_ref.shape)
    cnt_ref[...] = jnp.broadcast_to(total, cnt_ref.shape)

    vals = (i1 - float(ROUTER_LANE0), i2 - float(ROUTER_LANE0), rank1, rank2, gate1, gate2)
    info = jnp.zeros((tm, LANES), F32)
    for k, val in enumerate(vals):
        info = jnp.where(lane == float(k), val, info)
    info_ref[...] = info


def _router(x, w_r, b_r):
    t = x.shape[0]
    return pl.pallas_call(
        _router_kernel,
        grid=(t // ROUTER_TILE,),
        in_specs=[
            pl.BlockSpec((ROUTER_TILE, D_MODEL), lambda i: (i, 0)),
            _resident((D_MODEL, LANES)),
            _resident((1, LANES)),
        ],
        out_specs=[
            pl.BlockSpec((ROUTER_TILE, LANES), lambda i: (i, 0)),
            pl.BlockSpec((8, LANES), lambda i: (0, 0)),
        ],
        out_shape=[
            jax.ShapeDtypeStruct((t, LANES), F32),
            jax.ShapeDtypeStruct((8, LANES), F32),
        ],
        scratch_shapes=[pltpu.VMEM((8, LANES), F32), pltpu.VMEM((D_MODEL, 2 * LANES), BF16)],
        compiler_params=_cparams(1),
        name="router",
    )(x, w_r, b_r)


DISPATCH_TILE = 512
DMA_UNROLL = 8


def _dispatch_kernel(cnt_ref, pstart_ref, n_used_ref, dest_ref, xp_ref, xbuf_hbm, zero_ref, sem, zsem):
    step = pl.program_id(0)
    n_tok = dest_ref.shape[2] // TOP_K
    n_blocks = xbuf_hbm.shape[0] // MOE_TILE

    def token_copies(r):
        return [pltpu.make_async_copy(xp_ref.at[pl.ds(r, 1), :],
                                      xbuf_hbm.at[pl.ds(dest_ref[0, 0, TOP_K * r + k], 1), :], sem)
                for k in range(TOP_K)]

    def pad_rows(e):
        first = pstart_ref[e] + cnt_ref[e]
        n_pad = (-cnt_ref[e]) & (MOE_TILE - 1)
        return first, n_pad

    def zero_copy(row):
        return pltpu.make_async_copy(zero_ref.at[pl.ds(0, 1), :], xbuf_hbm.at[pl.ds(row, 1), :], zsem)

    def zero_block_copy(b):
        return pltpu.make_async_copy(zero_ref, xbuf_hbm.at[pl.ds(b * MOE_TILE, MOE_TILE), :], zsem)

    @pl.when(step == 0)
    def _():
        zero_ref[...] = jnp.zeros_like(zero_ref)

        def start_expert(e, carry):
            first, n_pad = pad_rows(e)
            lax.fori_loop(0, n_pad, lambda r, c: (zero_copy(first + r).start(), c)[1], 0)
            return carry
        lax.fori_loop(0, N_EXPERTS, start_expert, 0)
        lax.fori_loop(n_used_ref[0], n_blocks, lambda b, c: (zero_block_copy(b).start(), c)[1], 0)

    def start(g, carry):
        for u in range(DMA_UNROLL):
            for cp in token_copies(g * DMA_UNROLL + u):
                cp.start()
        return carry
    lax.fori_loop(0, n_tok // DMA_UNROLL, start, 0)

    def wait(g, carry):
        for u in range(DMA_UNROLL):
            for cp in token_copies(g * DMA_UNROLL + u):
                cp.wait()
        return carry
    lax.fori_loop(0, n_tok // DMA_UNROLL, wait, 0)

    @pl.when(step == 0)
    def _():
        def wait_expert(e, carry):
            first, n_pad = pad_rows(e)
            lax.fori_loop(0, n_pad, lambda r, c: (zero_copy(first + r).wait(), c)[1], 0)
            return carry
        lax.fori_loop(0, N_EXPERTS, wait_expert, 0)
        lax.fori_loop(n_used_ref[0], n_blocks, lambda b, c: (zero_block_copy(b).wait(), c)[1], 0)


def _dispatch(counts, pstart, n_used, dest, xp, n_rows):
    t = xp.shape[0]
    assert MOE_TILE & (MOE_TILE - 1) == 0 and t % DISPATCH_TILE == 0 and DISPATCH_TILE % DMA_UNROLL == 0
    dest_blocks = dest.reshape(t // DISPATCH_TILE, 1, TOP_K * DISPATCH_TILE)
    grid_spec = pltpu.PrefetchScalarGridSpec(
        num_scalar_prefetch=3,
        grid=(t // DISPATCH_TILE,),
        in_specs=[
            pl.BlockSpec((1, 1, TOP_K * DISPATCH_TILE), lambda i, c, p, n: (i, 0, 0), memory_space=pltpu.SMEM),
            pl.BlockSpec((DISPATCH_TILE, PACK_W), lambda i, c, p, n: (i, 0)),
        ],
        out_specs=pl.BlockSpec(memory_space=pl.ANY),
        scratch_shapes=[
            pltpu.VMEM((MOE_TILE, PACK_W), jnp.uint32),
            pltpu.SemaphoreType.DMA(()),
            pltpu.SemaphoreType.DMA(()),
        ],
    )
    return pl.pallas_call(
        _dispatch_kernel,
        grid_spec=grid_spec,
        out_shape=jax.ShapeDtypeStruct((n_rows, PACK_W), jnp.uint32),
        compiler_params=_cparams(1),
        name="dispatch",
    )(counts, pstart, n_used, dest_blocks, xp)


def _expert_kernel(blk_e_ref, n_used_ref, first_ref, slot_ref, next_e_ref, x_ref, w1_hbm, w3_hbm, w2_hbm,
                   y_ref, w1f_ref, w3f_ref, w2f_ref, sem, w1b_ref, w3b_ref, w2b_ref, *, layer):
    b = pl.program_id(0)

    def weight_copies(e, s):
        return [pltpu.make_async_copy(w_hbm.at[layer, e], wf_ref.at[s], sem.at[s])
                for w_hbm, wf_ref in ((w1_hbm, w1f_ref), (w3_hbm, w3f_ref), (w2_hbm, w2f_ref))]

    @pl.when(b < n_used_ref[0])
    def _():
        @pl.when(first_ref[b] == 1)
        def _():
            s = slot_ref[b]

            @pl.when(b == 0)
            def _():
                for cp in weight_copies(blk_e_ref[0], s):
                    cp.start()

            for cp in weight_copies(blk_e_ref[b], s):
                cp.wait()

            @pl.when(next_e_ref[b] >= 0)
            def _():
                for cp in weight_copies(next_e_ref[b], 1 - s):
                    cp.start()

            w1b_ref[...] = w1f_ref[s].astype(BF16)
            w3b_ref[...] = w3f_ref[s].astype(BF16)
            w2b_ref[...] = w2f_ref[s].astype(BF16)

        xb = _unpack_rows(x_ref[...]).astype(BF16)
        h1 = _dot(xb, w1b_ref[...])
        h3 = _dot(xb, w3b_ref[...])
        h = (h1 * jax.nn.sigmoid(h1)) * h3
        y_ref[...] = _pack_rows(_dot(h.astype(BF16), w2b_ref[...]))

    @pl.when(b >= n_used_ref[0])
    def _():
        y_ref[...] = jnp.zeros_like(y_ref)


def _experts(xbuf, blk_e, n_used, first, slot, next_e, layer, w1, w3, w2):
    n_blocks = xbuf.shape[0] // MOE_TILE

    def x_index(b, be, nu, *_):
        return (jnp.minimum(b, jnp.maximum(nu[0] - 1, 0)), 0)

    grid_spec = pltpu.PrefetchScalarGridSpec(
        num_scalar_prefetch=5,
        grid=(n_blocks,),
        in_specs=[
            pl.BlockSpec((MOE_TILE, PACK_W), x_index),
            pl.BlockSpec(memory_space=pl.ANY),
            pl.BlockSpec(memory_space=pl.ANY),
            pl.BlockSpec(memory_space=pl.ANY),
        ],
        out_specs=pl.BlockSpec((MOE_TILE, PACK_W), lambda b, *_: (b, 0)),
        scratch_shapes=[
            pltpu.VMEM((2, D_MODEL, EXPERT_FF), F32),
            pltpu.VMEM((2, D_MODEL, EXPERT_FF), F32),
            pltpu.VMEM((2, EXPERT_FF, D_MODEL), F32),
            pltpu.SemaphoreType.DMA((2,)),
            pltpu.VMEM((D_MODEL, EXPERT_FF), BF16),
            pltpu.VMEM((D_MODEL, EXPERT_FF), BF16),
            pltpu.VMEM((EXPERT_FF, D_MODEL), BF16),
        ],
    )
    return pl.pallas_call(
        functools.partial(_expert_kernel, layer=layer),
        grid_spec=grid_spec,
        out_shape=jax.ShapeDtypeStruct(xbuf.shape, jnp.uint32),
        compiler_params=_cparams(1),
        name="experts",
    )(blk_e, n_used, first, slot, next_e, xbuf, w1, w3, w2)


def _combine_kernel(dest_ref, dest_next_ref, info_ref, x_ref, y_hbm, g_ref, b_ref, o_ref, rows_ref, sem):
    i = pl.program_id(0)
    n_steps = pl.num_programs(0)
    tm = x_ref.shape[0]
    slot = lax.rem(i, 2)

    def row_copies(idx_ref, s, r):
        return [pltpu.make_async_copy(y_hbm.at[pl.ds(idx_ref[0, 0, TOP_K * r + k], 1), :],
                                      rows_ref.at[s, k, pl.ds(r, 1), :], sem.at[s])
                for k in range(TOP_K)]

    def start_tile(idx_ref, s):
        def body(g, carry):
            for u in range(DMA_UNROLL):
                for cp in row_copies(idx_ref, s, g * DMA_UNROLL + u):
                    cp.start()
            return carry
        lax.fori_loop(0, tm // DMA_UNROLL, body, 0)

    @pl.when(i == 0)
    def _():
        start_tile(dest_ref, 0)

    @pl.when(i + 1 < n_steps)
    def _():
        start_tile(dest_next_ref, 1 - slot)

    def wait(g, carry):
        for u in range(DMA_UNROLL):
            for cp in row_copies(dest_ref, slot, g * DMA_UNROLL + u):
                cp.wait()
        return carry
    lax.fori_loop(0, tm // DMA_UNROLL, wait, 0)

    info = info_ref[...]
    f = None
    for k in range(TOP_K):
        term = info[:, 4 + k:5 + k] * _unpack_rows(rows_ref[slot, k])
        f = term if f is None else f + term
    y = DN_ALPHA * x_ref[...] + f
    o_ref[...] = _layer_norm(y, g_ref[...], b_ref[...])


def _combine(dest, info, x, yb, ln_g, ln_b):
    t = x.shape[0]
    n_steps = t // ROW_TILE
    assert ROW_TILE % DMA_UNROLL == 0
    dest_blocks = dest.reshape(n_steps, 1, TOP_K * ROW_TILE)
    idx_block = (1, 1, TOP_K * ROW_TILE)
    return pl.pallas_call(
        _combine_kernel,
        grid=(n_steps,),
        in_specs=[
            pl.BlockSpec(idx_block, lambda i: (i, 0, 0), memory_space=pltpu.SMEM),
            pl.BlockSpec(idx_block, lambda i: (jnp.minimum(i + 1, n_steps - 1), 0, 0), memory_space=pltpu.SMEM),
            pl.BlockSpec((ROW_TILE, LANES), lambda i: (i, 0)),
            pl.BlockSpec((ROW_TILE, D_MODEL), lambda i: (i, 0)),
            pl.BlockSpec(memory_space=pl.ANY),
            _resident((1, D_MODEL)),
            _resident((1, D_MODEL)),
        ],
        out_specs=pl.BlockSpec((ROW_TILE, D_MODEL), lambda i: (i, 0)),
        out_shape=jax.ShapeDtypeStruct((t, D_MODEL), F32),
        scratch_shapes=[
            pltpu.VMEM((2, TOP_K, ROW_TILE, PACK_W), jnp.uint32),
            pltpu.SemaphoreType.DMA((2,)),
        ],
        compiler_params=_cparams(1),
        name="combine",
    )(dest_blocks, dest_blocks, info, x, yb, ln_g, ln_b)


def _moe(x, xp, wg1, bg1, wg2, bg2, layer, w1, w3, w2, ln_g, ln_b):
    t = x.shape[0]
    w_r = jnp.concatenate([wg1, jnp.transpose(wg2, (1, 0, 2)).reshape(D_MODEL, N_EXPERTS)], axis=1)
    w_r = jnp.pad(w_r, ((0, 0), (0, LANES - w_r.shape[1])))
    b_r = jnp.pad(jnp.concatenate([bg1, bg2.reshape(-1)]), (0, LANES - N_GROUPS - N_EXPERTS))[None, :]
    info, cnt = _router(x, w_r, b_r)

    expert = info[:, 0:2].astype(jnp.int32)
    rank = info[:, 2:4].astype(jnp.int32)
    counts = cnt[0, ROUTER_LANE0:ROUTER_LANE0 + N_EXPERTS].astype(jnp.int32)
    padded = ((counts + MOE_TILE - 1) // MOE_TILE) * MOE_TILE
    pend = jnp.cumsum(padded)
    pstart = pend - padded
    experts = jnp.arange(N_EXPERTS, dtype=jnp.int32)
    dest = jnp.sum(jnp.where(expert[:, :, None] == experts, pstart, 0), axis=-1) + rank
    n_blocks = (t * TOP_K) // MOE_TILE + N_EXPERTS
    blk_start = jnp.arange(n_blocks, dtype=jnp.int32) * MOE_TILE
    blk_e = jnp.minimum(jnp.sum(pend[None, :] <= blk_start[:, None], axis=1), N_EXPERTS - 1).astype(jnp.int32)
    n_used = (pend[-1:] // MOE_TILE).astype(jnp.int32)
    first = jnp.concatenate([jnp.ones((1,), jnp.int32), (blk_e[1:] != blk_e[:-1]).astype(jnp.int32)])
    slot = (jnp.cumsum(first) - 1) % 2
    later_used = (counts[None, :] > 0) & (experts[None, :] > experts[:, None])
    next_used = jnp.min(jnp.where(later_used, experts[None, :], N_EXPERTS), axis=1)
    next_e = jnp.where(next_used < N_EXPERTS, next_used, -1)[blk_e].astype(jnp.int32)

    xbuf = _dispatch(counts, pstart.astype(jnp.int32), n_used, dest, xp, n_blocks * MOE_TILE)
    yb = _experts(xbuf, blk_e, n_used, first, slot.astype(jnp.int32), next_e, layer, w1, w3, w2)
    return _combine(dest, info, x, yb, ln_g, ln_b)


def kernel(x, mem, a_w_in, a_sgu_ln_g, a_sgu_ln_b, a_ws, a_bs, a_w_out, b_w_in, b_lambda, b_subln_g, b_w_out, shared_w_kv, rel_bias, mem_w_kv, ln_g, ln_b, moe_wg1, moe_bg1, moe_wg2, moe_bg2, moe_w1, moe_w3, moe_w2):
    b_, s_, d_ = x.shape
    assert b_ == 1 and d_ == D_MODEL
    h = x.reshape(s_, d_)
    kt_mem, v_mem = _memkv(mem.reshape(MEM_TOKENS, d_), mem_w_kv)
    bias_table = _near_bias_table(rel_bias)
    shared_k = shared_v = None
    for l in range(DEPTH):
        if l < N_A_LAYERS:
            i = l
            mix, qm = _pre_a(h, a_w_in[i].astype(BF16), a_sgu_ln_g[i][None, :], a_sgu_ln_b[i][None, :],
                             a_ws[i], jnp.transpose(a_bs[i]))
            w_out = a_w_out[i]
        else:
            i = l - N_A_LAYERS
            qd, qm = _pre_b(h, jnp.transpose(b_w_in[i][:, :DIFF_QK_W]).astype(BF16),
                            b_w_in[i][:, DIFF_QK_W:].astype(BF16))
            lambda_init = 0.8 - 0.6 * math.exp(-0.3 * l)
            mix = _attention(qd, shared_k, shared_v, bias_table, b_lambda[i], b_subln_g[i][None, :],
                             lambda_init)
            w_out = b_w_out[i]
        h, hp = _post(mix, qm, kt_mem, v_mem, l, h, w_out.astype(BF16), ln_g[l, 0][None, :], ln_b[l, 0][None, :])
        h = _moe(h, hp, moe_wg1[l], moe_bg1[l], moe_wg2[l], moe_bg2[l], l, moe_w1, moe_w3, moe_w2,
                 ln_g[l, 1][None, :], ln_b[l, 1][None, :])
        if l == N_A_LAYERS - 1:
            shared_k, shared_v = _kvproj(h, shared_w_kv[:, :DIFF_QK_W].astype(BF16),
                                         jnp.transpose(shared_w_kv[:, DIFF_QK_W:]).astype(BF16))
    return h.reshape(b_, s_, d_)
```

```python
import functools
import math

import jax
import jax.numpy as jnp
from jax import lax
from jax.experimental import pallas as pl
from jax.experimental.pallas import tpu as pltpu

F32 = jnp.float32
BF16 = jnp.bfloat16

D_MODEL = 2048
DEPTH = 4
CHUNK = 64
N_A_LAYERS = DEPTH // 2
HEAD_DIM = 128
MEM_TOKENS = 256
MEM_HEADS = 4
MEM_W = MEM_HEADS * HEAD_DIM
MIX_W = D_MODEL - MEM_W
SGU_CHUNK = 128
SGU_GROUPS = 4
SGU_GROUP_W = MIX_W // SGU_GROUPS
DIFF_HEADS = MIX_W // (2 * HEAD_DIM)
DIFF_QK_W = 2 * DIFF_HEADS * HEAD_DIM
DIFF_V_DIM = 2 * HEAD_DIM
REL_BUCKETS = 32
REL_MAX_DIST = 128
N_GROUPS = 4
EXPERTS_PER_GROUP = 8
N_EXPERTS = N_GROUPS * EXPERTS_PER_GROUP
TOP_K = 2
EXPERT_FF = 512
DN_ALPHA = (2 * DEPTH) ** 0.25
LN_EPS = 1e-5
QK_SCALE = HEAD_DIM ** -0.5
SQRT_HALF = math.sqrt(0.5)
LOG2E = math.log2(math.e)
NEG_BIG = -1e30

LANES = 128
VMEM_LIMIT_BYTES = 56 * 1024 * 1024

ROW_TILE = 256
ROUTER_TILE = 512
ATT_TILE = 256
ATT_FAR = 2
ATT_NEAR = ATT_FAR + 1
MOE_TILE = 128
ROUTER_LANE0 = N_GROUPS


def _cparams(n_axes=1):
    return pltpu.CompilerParams(
        dimension_semantics=("arbitrary",) * n_axes,
        vmem_limit_bytes=VMEM_LIMIT_BYTES,
    )


def _dot(a, b):
    return jnp.dot(a, b, preferred_element_type=F32)


def _dot_nt(a, b):
    return lax.dot_general(a, b, (((1,), (1,)), ((), ())), preferred_element_type=F32)


def _gelu(x):
    return 0.5 * x * (1.0 + lax.erf(x * SQRT_HALF))


def _layer_norm(y, g, b):
    mu = jnp.mean(y, axis=-1, keepdims=True)
    d = y - mu
    var = jnp.mean(d * d, axis=-1, keepdims=True)
    return d * lax.rsqrt(var + LN_EPS) * g + b


HI16 = 0xFFFF0000
PACK_W = D_MODEL // 2


def _pack_rows(y):
    lo = y[:, :PACK_W].astype(BF16).astype(F32)
    hi = y[:, PACK_W:].astype(BF16).astype(F32)
    return (pltpu.bitcast(lo, jnp.uint32) >> 16) | (pltpu.bitcast(hi, jnp.uint32) & jnp.uint32(HI16))


def _unpack_rows(w):
    lo = pltpu.bitcast(w << 16, F32)
    hi = pltpu.bitcast(w & jnp.uint32(HI16), F32)
    return jnp.concatenate([lo, hi], axis=-1)


def _resident(shape):
    nd = len(shape)
    return pl.BlockSpec(shape, lambda *_: (0,) * nd, pipeline_mode=pl.Buffered(1))


def _memkv_kernel(mem_ref, w_ref, kt_ref, v_ref):
    kv = _dot(mem_ref[...].astype(BF16), w_ref[0].astype(BF16))
    kt_ref[0] = kv[:, :MEM_W].T.astype(BF16)
    v_ref[0] = kv[:, MEM_W:].astype(BF16)


def _memkv(mem, mem_w_kv):
    n_layers = mem_w_kv.shape[0]
    m = mem.shape[0]
    return pl.pallas_call(
        _memkv_kernel,
        grid=(n_layers,),
        in_specs=[
            pl.BlockSpec((m, D_MODEL), lambda l: (0, 0)),
            pl.BlockSpec((1, D_MODEL, 2 * MEM_W), lambda l: (l, 0, 0)),
        ],
        out_specs=[
            pl.BlockSpec((1, MEM_W, m), lambda l: (l, 0, 0)),
            pl.BlockSpec((1, m, MEM_W), lambda l: (l, 0, 0)),
        ],
        out_shape=[
            jax.ShapeDtypeStruct((n_layers, MEM_W, m), BF16),
            jax.ShapeDtypeStruct((n_layers, m, MEM_W), BF16),
        ],
        compiler_params=_cparams(1),
        name="memkv",
    )(mem, mem_w_kv)


def _pre_a_kernel(x_ref, w_ref, lng_ref, lnb_ref, ws_ref, bst_ref, mix_ref, qm_ref, vn_ref):
    tm = x_ref.shape[0]
    x = x_ref[...].astype(BF16)
    v = _gelu(_dot(x, w_ref[:, MIX_W:2 * MIX_W]))
    vn_ref[...] = _layer_norm(v, lng_ref[...], lnb_ref[...]).astype(BF16)
    row = lax.broadcasted_iota(jnp.int32, (SGU_CHUNK, SGU_CHUNK), 0)
    col = lax.broadcasted_iota(jnp.int32, (SGU_CHUNK, SGU_CHUNK), 1)
    shift = CHUNK.bit_length() - 1
    keep = (col >> shift) <= (row >> shift)
    for g in range(SGU_GROUPS):
        cols = slice(g * SGU_GROUP_W, (g + 1) * SGU_GROUP_W)
        w_sp = jnp.where(keep, ws_ref[g], 0.0).astype(BF16)
        u = _gelu(_dot(x, w_ref[:, cols]))
        bias = bst_ref[:, g:g + 1]
        for c in range(tm // SGU_CHUNK):
            rows = slice(c * SGU_CHUNK, (c + 1) * SGU_CHUNK)
            gate = _dot(w_sp, vn_ref[rows, cols]) + bias
            mix_ref[rows, cols] = (u[rows] * gate).astype(BF16)
    qm_ref[...] = (_dot(x, w_ref[:, 2 * MIX_W:]) * QK_SCALE).astype(BF16)


def _pre_a(x, w_in, ln_g, ln_b, ws, bs_t):
    t = x.shape[0]
    n_in = w_in.shape[1]
    return pl.pallas_call(
        _pre_a_kernel,
        grid=(t // ROW_TILE,),
        in_specs=[
            pl.BlockSpec((ROW_TILE, D_MODEL), lambda i: (i, 0)),
            _resident((D_MODEL, n_in)),
            _resident((1, MIX_W)),
            _resident((1, MIX_W)),
            _resident((SGU_GROUPS, SGU_CHUNK, SGU_CHUNK)),
            _resident((SGU_CHUNK, SGU_GROUPS)),
        ],
        out_specs=[
            pl.BlockSpec((ROW_TILE, MIX_W), lambda i: (i, 0)),
            pl.BlockSpec((ROW_TILE, MEM_W), lambda i: (i, 0)),
        ],
        out_shape=[
            jax.ShapeDtypeStruct((t, MIX_W), BF16),
            jax.ShapeDtypeStruct((t, MEM_W), BF16),
        ],
        scratch_shapes=[pltpu.VMEM((ROW_TILE, MIX_W), BF16)],
        compiler_params=_cparams(1),
        name="pre_a",
    )(x, w_in, ln_g, ln_b, ws, bs_t)


def _pre_b_kernel(x_ref, wqt_ref, wm_ref, qt_ref, qm_ref):
    x = x_ref[...].astype(BF16)
    for h in range(DIFF_HEADS):
        rows = slice(h * 2 * HEAD_DIM, (h + 1) * 2 * HEAD_DIM)
        qt_ref[h, 0] = (_dot_nt(wqt_ref[rows, :], x) * (QK_SCALE * LOG2E)).astype(BF16)
    qm_ref[...] = (_dot(x, wm_ref[...]) * QK_SCALE).astype(BF16)


def _pre_b(x, w_q_t, w_mem):
    t = x.shape[0]
    return pl.pallas_call(
        _pre_b_kernel,
        grid=(t // ATT_TILE,),
        in_specs=[
            pl.BlockSpec((ATT_TILE, D_MODEL), lambda i: (i, 0)),
            _resident((DIFF_QK_W, D_MODEL)),
            _resident((D_MODEL, MEM_W)),
        ],
        out_specs=[
            pl.BlockSpec((DIFF_HEADS, 1, 2 * HEAD_DIM, ATT_TILE), lambda i: (0, i, 0, 0)),
            pl.BlockSpec((ATT_TILE, MEM_W), lambda i: (i, 0)),
        ],
        out_shape=[
            jax.ShapeDtypeStruct((DIFF_HEADS, t // ATT_TILE, 2 * HEAD_DIM, ATT_TILE), BF16),
            jax.ShapeDtypeStruct((t, MEM_W), BF16),
        ],
        compiler_params=_cparams(1),
        name="pre_b",
    )(x, w_q_t, w_mem)


def _kvproj_kernel(x_ref, wk_ref, wvt_ref, k_ref, vt_ref):
    x = x_ref[...].astype(BF16)
    k_ref[...] = _dot(x, wk_ref[...]).astype(BF16)
    for h in range(DIFF_HEADS):
        rows = slice(h * DIFF_V_DIM, (h + 1) * DIFF_V_DIM)
        vt_ref[h, 0] = _dot_nt(wvt_ref[rows, :], x).astype(BF16)


def _kvproj(x, w_k, w_v_t):
    t = x.shape[0]
    n_v = DIFF_HEADS * DIFF_V_DIM
    return pl.pallas_call(
        _kvproj_kernel,
        grid=(t // ATT_TILE,),
        in_specs=[
            pl.BlockSpec((ATT_TILE, D_MODEL), lambda i: (i, 0)),
            _resident((D_MODEL, DIFF_QK_W)),
            _resident((n_v, D_MODEL)),
        ],
        out_specs=[
            pl.BlockSpec((ATT_TILE, DIFF_QK_W), lambda i: (i, 0)),
            pl.BlockSpec((DIFF_HEADS, 1, DIFF_V_DIM, ATT_TILE), lambda i: (0, i, 0, 0)),
        ],
        out_shape=[
            jax.ShapeDtypeStruct((t, DIFF_QK_W), BF16),
            jax.ShapeDtypeStruct((DIFF_HEADS, t // ATT_TILE, DIFF_V_DIM, ATT_TILE), BF16),
        ],
        compiler_params=_cparams(1),
        name="kvproj",
    )(x, w_k, w_v_t)


def _attn_kernel(qt_ref, k_ref, vt_ref, bias_ref, lam_ref, g_ref, o_ref, acc1_ref, acc2_ref,
                 s_ref, p_ref, *, lambda_init):
    assert ATT_FAR in (1, 2) and ATT_NEAR == ATT_FAR + 1
    i = pl.program_id(1)
    tq = qt_ref.shape[3]
    acc1_ref[...] = jnp.zeros_like(acc1_ref)
    acc2_ref[...] = jnp.zeros_like(acc2_ref)

    def scores(j0, n_blk, slot):
        start = pl.multiple_of(j0 * tq, tq)
        kblk = k_ref[pl.ds(start, n_blk * tq), :]
        for s in range(2):
            rows = slice(s * HEAD_DIM, (s + 1) * HEAD_DIM)
            s_ref[slot, s, :n_blk * tq, :] = _dot(kblk[:, rows], qt_ref[0, 0, rows, :])

    def probs(n_blk, bias, stats, slot):
        new_stats, alphas = [], []
        for s in range(2):
            m, l = stats[2 * s], stats[2 * s + 1]
            st = s_ref[slot, s, :n_blk * tq, :]
            if bias is not None:
                st = st + bias
            m_new = jnp.maximum(m, jnp.max(st, axis=0, keepdims=True))
            alpha = jnp.exp2(m - m_new)
            p = jnp.exp2(st - m_new)
            p_ref[slot, s, :n_blk * tq, :] = p.astype(BF16)
            new_stats += [m_new, alpha * l + jnp.sum(p, axis=0, keepdims=True)]
            alphas.append(alpha)
        return tuple(new_stats), tuple(alphas)

    def accumulate(j0, n_blk, alphas, slot):
        vts = [vt_ref[0, j0 + n] for n in range(n_blk)]
        for s, acc_ref in enumerate((acc1_ref, acc2_ref)):
            pv = _dot(vts[0], p_ref[slot, s, :tq, :])
            for n in range(1, n_blk):
                pv = pv + _dot(vts[n], p_ref[slot, s, n * tq:(n + 1) * tq, :])
            acc_ref[...] = alphas[s] * acc_ref[...] + pv

    neg = jnp.full((1, tq), NEG_BIG, F32)
    zero = jnp.zeros((1, tq), F32)
    stats = (neg, zero, neg, zero)
    fb = ATT_FAR
    n_far = jnp.maximum(i - 1, 0) // fb
    near_j0 = jnp.maximum(i + 1 - ATT_NEAR, 0)
    near_slot = 2
    near_bias = bias_ref[0, 0]

    def near_scores():
        scores(near_j0, ATT_NEAR, near_slot)

    def near_probs(stats):
        return probs(ATT_NEAR, near_bias, stats, near_slot)

    def only_near(stats):
        near_scores()
        stats, alphas = near_probs(stats)
        accumulate(near_j0, ATT_NEAR, alphas, near_slot)
        return stats

    def far_then_near(stats):
        scores(0, fb, 0)

        def first_two(stats):
            out = probs(fb, None, stats, 0)
            scores(fb, fb, 1)
            return out

        def first_and_near(stats):
            out = probs(fb, None, stats, 0)
            near_scores()
            return out
        carry = lax.cond(n_far >= 2, first_two, first_and_near, stats)

        def steady(t, carry, parity):
            stats, alphas = carry
            accumulate(fb * (t - 2), fb, alphas, parity)
            out = probs(fb, None, stats, 1 - parity)
            scores(fb * t, fb, parity)
            return out

        def body(u, carry):
            t = 2 + 2 * u
            return steady(t + 1, steady(t, carry, 0), 1)
        n_steady = jnp.maximum(n_far - 2, 0)
        carry = lax.fori_loop(0, n_steady // 2, body, carry)
        carry = lax.cond(lax.rem(n_steady, 2) == 1,
                         lambda c: steady(n_far - 1, c, 0), lambda c: c, carry)

        def last_far_and_near(carry):
            stats, alphas = carry
            accumulate(fb * (n_far - 2), fb, alphas, lax.rem(n_far, 2))
            out = probs(fb, None, stats, lax.rem(n_far - 1, 2))
            near_scores()
            return out
        stats, alphas = lax.cond(n_far >= 2, last_far_and_near, lambda c: c, carry)
        accumulate(fb * (n_far - 1), fb, alphas, lax.rem(n_far - 1, 2))
        stats, alphas = near_probs(stats)
        accumulate(near_j0, ATT_NEAR, alphas, near_slot)
        return stats

    m1, l1, m2, l2 = lax.cond(n_far > 0, far_then_near, only_near, stats)

    lp = lam_ref[...]
    lam = (jnp.exp(jnp.sum(lp[0:1] * lp[1:2], axis=-1, keepdims=True))
           - jnp.exp(jnp.sum(lp[2:3] * lp[3:4], axis=-1, keepdims=True)) + lambda_init)
    ot = acc1_ref[...] / l1 - lam * (acc2_ref[...] / l2)
    ms = jnp.mean(ot * ot, axis=0, keepdims=True)
    ot = ot * lax.rsqrt(ms + LN_EPS)
    o_ref[...] = (ot.T * (g_ref[...] * (1.0 - lambda_init))).astype(o_ref.dtype)


def _attention(qt, k, vt, bias, lam_params, subln_g, lambda_init):
    t = k.shape[0]
    n_blk = t // ATT_TILE
    kern = functools.partial(_attn_kernel, lambda_init=lambda_init)
    return pl.pallas_call(
        kern,
        grid=(DIFF_HEADS, n_blk),
        in_specs=[
            pl.BlockSpec((1, 1, 2 * HEAD_DIM, ATT_TILE), lambda h, i: (h, i, 0, 0)),
            pl.BlockSpec((t, 2 * HEAD_DIM), lambda h, i: (0, h)),
            pl.BlockSpec((1, n_blk, DIFF_V_DIM, ATT_TILE), lambda h, i: (h, 0, 0, 0)),
            pl.BlockSpec((1, 1, ATT_NEAR * ATT_TILE, ATT_TILE),
                         lambda h, i: (h, _near_table_variant(i), 0, 0)),
            pl.BlockSpec((4, HEAD_DIM), lambda h, i: (0, 0)),
            pl.BlockSpec((1, DIFF_V_DIM), lambda h, i: (0, 0)),
        ],
        out_specs=pl.BlockSpec((ATT_TILE, DIFF_V_DIM), lambda h, i: (i, h)),
        out_shape=jax.ShapeDtypeStruct((t, DIFF_HEADS * DIFF_V_DIM), BF16),
        scratch_shapes=[pltpu.VMEM((DIFF_V_DIM, ATT_TILE), F32),
                        pltpu.VMEM((DIFF_V_DIM, ATT_TILE), F32),
                        pltpu.VMEM((3, 2, ATT_NEAR * ATT_TILE, ATT_TILE), F32),
                        pltpu.VMEM((3, 2, ATT_NEAR * ATT_TILE, ATT_TILE), BF16)],
        compiler_params=_cparams(2),
        name="diff_attn",
    )(qt, k, vt, bias, lam_params, subln_g)


def _relative_bucket(rel):
    n = REL_BUCKETS // 2
    max_exact = n // 2
    ret = jnp.where(rel > 0, n, 0)
    a = jnp.abs(rel)
    af = jnp.maximum(a, 1).astype(jnp.float32)
    large = max_exact + (jnp.log(af / max_exact) / math.log(REL_MAX_DIST / max_exact)
                         * (n - max_exact)).astype(jnp.int32)
    large = jnp.minimum(large, n - 1)
    return ret + jnp.where(a < max_exact, a, large)


def _near_table_variant(i):
    return jnp.where(i < ATT_NEAR - 1, i, ATT_NEAR - 1 + lax.rem(i - (ATT_NEAR - 1), ATT_FAR))


def _near_bias_tables(rel_bias):
    tq = ATT_TILE
    assert tq >= REL_MAX_DIST and tq % CHUNK == 0 and ATT_NEAR == ATT_FAR + 1
    qpos = jnp.arange(tq)[:, None]
    kpos = jnp.arange(ATT_NEAR * tq)[None, :] - (ATT_NEAR - 1) * tq
    onehot = jax.nn.one_hot(_relative_bucket(kpos - qpos), REL_BUCKETS, dtype=F32)
    bias = jnp.einsum('qkb,bh->qkh', onehot, rel_bias.astype(F32),
                      precision=lax.Precision.HIGHEST)
    far = rel_bias[_relative_bucket(jnp.array(-ATT_NEAR * tq))].astype(F32)
    visible = (kpos // CHUNK) <= (qpos // CHUNK)
    table = jnp.where(visible[:, :, None], (bias - far) * LOG2E, NEG_BIG)
    base = jnp.transpose(table, (2, 1, 0))
    masked = jnp.full_like(base[:, :tq], NEG_BIG)
    variants = []
    for i in range(ATT_NEAR - 1):
        lead = ATT_NEAR - 1 - i
        variants.append(jnp.concatenate([base[:, lead * tq:]] + [masked] * lead, axis=1))
    for covered in range(ATT_FAR):
        variants.append(jnp.concatenate([masked] * covered + [base[:, covered * tq:]], axis=1))
    return jnp.stack(variants, axis=1)


def _post_kernel(mix_ref, qm_ref, kt_ref, vm_ref, x_ref, w_ref, g_ref, b_ref, o_ref, op_ref):
    heads = []
    for h in range(MEM_HEADS):
        cols = slice(h * HEAD_DIM, (h + 1) * HEAD_DIM)
        s = _dot(qm_ref[:, cols], kt_ref[0, cols, :])
        p = jnp.exp(s - jnp.max(s, axis=-1, keepdims=True))
        l = jnp.sum(p, axis=-1, keepdims=True)
        heads.append((_dot(p.astype(BF16), vm_ref[0, :, cols]) / l).astype(BF16))
    mem_out = jnp.concatenate(heads, axis=-1)
    t = _dot(mix_ref[...], w_ref[:MIX_W, :]) + _dot(mem_out, w_ref[MIX_W:, :])
    y = _layer_norm(DN_ALPHA * x_ref[...] + t, g_ref[...], b_ref[...])
    o_ref[...] = y
    op_ref[...] = _pack_rows(y)


def _post(mix, qm, kt, vm, layer, x, w_out, ln_g, ln_b):
    t = x.shape[0]
    m = kt.shape[2]
    return pl.pallas_call(
        _post_kernel,
        grid=(t // ROW_TILE,),
        in_specs=[
            pl.BlockSpec((ROW_TILE, MIX_W), lambda i: (i, 0)),
            pl.BlockSpec((ROW_TILE, MEM_W), lambda i: (i, 0)),
            pl.BlockSpec((1, MEM_W, m), lambda i: (layer, 0, 0), pipeline_mode=pl.Buffered(1)),
            pl.BlockSpec((1, m, MEM_W), lambda i: (layer, 0, 0), pipeline_mode=pl.Buffered(1)),
            pl.BlockSpec((ROW_TILE, D_MODEL), lambda i: (i, 0)),
            _resident((D_MODEL, D_MODEL)),
            _resident((1, D_MODEL)),
            _resident((1, D_MODEL)),
        ],
        out_specs=[
            pl.BlockSpec((ROW_TILE, D_MODEL), lambda i: (i, 0)),
            pl.BlockSpec((ROW_TILE, PACK_W), lambda i: (i, 0)),
        ],
        out_shape=[
            jax.ShapeDtypeStruct((t, D_MODEL), F32),
            jax.ShapeDtypeStruct((t, PACK_W), jnp.uint32),
        ],
        compiler_params=_cparams(1),
        name="post",
    )(mix, qm, kt, vm, x, w_out, ln_g, ln_b)


def _router_kernel(x_ref, w_ref, b_ref, info_ref, cnt_ref, carry_ref, wsplit_ref):
    step = pl.program_id(0)
    tm = x_ref.shape[0]

    @pl.when(step == 0)
    def _():
        carry_ref[...] = jnp.zeros_like(carry_ref)
        w = w_ref[...]
        wh = w.astype(BF16)
        wsplit_ref[:, :LANES] = wh
        wsplit_ref[:, LANES:] = (w - wh.astype(F32)).astype(BF16)

    x = x_ref[...]
    xh = x.astype(BF16)
    xl = (x - xh.astype(F32)).astype(BF16)
    both = _dot(xh, wsplit_ref[...])
    logits = both[:, :LANES] + (both[:, LANES:] + _dot(xl, wsplit_ref[:, :LANES])) + b_ref[...]
    lane = lax.broadcasted_iota(jnp.int32, (tm, LANES), 1).astype(F32)
    no_lane = float(LANES)

    def top(mask):
        val = jnp.max(jnp.where(mask, logits, -jnp.inf), axis=-1, keepdims=True)
        idx = jnp.min(jnp.where(mask & (logits == val), lane, no_lane), axis=-1, keepdims=True)
        return val, idx

    is_group = lane < float(N_GROUPS)
    g_val, g_idx = top(is_group)
    p_sel = 1.0 / jnp.sum(jnp.where(is_group, jnp.exp(logits - g_val), 0.0), axis=-1, keepdims=True)
    first = float(ROUTER_LANE0) + float(EXPERTS_PER_GROUP) * g_idx
    in_group = (lane >= first) & (lane < first + float(EXPERTS_PER_GROUP))
    v1, i1 = top(in_group)
    v2, i2 = top(in_group & (lane != i1))
    e2 = jnp.exp(v2 - v1)
    gate1 = p_sel / (1.0 + e2)
    gate2 = p_sel * e2 / (1.0 + e2)

    hit1 = lane == i1
    hit2 = lane == i2
    onehot = jnp.where(hit1 | hit2, 1.0, 0.0)
    r = lax.broadcasted_iota(jnp.int32, (tm, tm), 0)
    c = lax.broadcasted_iota(jnp.int32, (tm, tm), 1)
    strict_lower = jnp.where(c < r, 1.0, 0.0).astype(BF16)
    before = _dot(strict_lower, onehot.astype(BF16)) + carry_ref[0:1, :]
    rank1 = jnp.sum(jnp.where(hit1, before, 0.0), axis=-1, keepdims=True)
    rank2 = jnp.sum(jnp.where(hit2, before, 0.0), axis=-1, keepdims=True)
    total = carry_ref[0:1, :] + jnp.sum(onehot, axis=0, keepdims=True)
    carry_ref[...] = jnp.broadcast_to(total, carry_ref.shape)
    cnt_ref[...] = jnp.broadcast_to(total, cnt_ref.shape)

    vals = (i1 - float(ROUTER_LANE0), i2 - float(ROUTER_LANE0), rank1, rank2, gate1, gate2)
    info = jnp.zeros((tm, LANES), F32)
    for k, val in enumerate(vals):
        info = jnp.where(lane == float(k), val, info)
    info_ref[...] = info


def _router(x, w_r, b_r):
    t = x.shape[0]
    return pl.pallas_call(
        _router_kernel,
        grid=(t // ROUTER_TILE,),
        in_specs=[
            pl.BlockSpec((ROUTER_TILE, D_MODEL), lambda i: (i, 0)),
            _resident((D_MODEL, LANES)),
            _resident((1, LANES)),
        ],
        out_specs=[
            pl.BlockSpec((ROUTER_TILE, LANES), lambda i: (i, 0)),
            pl.BlockSpec((8, LANES), lambda i: (0, 0)),
        ],
        out_shape=[
            jax.ShapeDtypeStruct((t, LANES), F32),
            jax.ShapeDtypeStruct((8, LANES), F32),
        ],
        scratch_shapes=[pltpu.VMEM((8, LANES), F32), pltpu.VMEM((D_MODEL, 2 * LANES), BF16)],
        compiler_params=_cparams(1),
        name="router",
    )(x, w_r, b_r)


DISPATCH_TILE = 512
DMA_UNROLL = 8


def _dispatch_kernel(cnt_ref, pstart_ref, n_used_ref, dest_ref, xp_ref, xbuf_hbm, zero_ref, sem, zsem):
    step = pl.program_id(0)
    n_tok = dest_ref.shape[2] // TOP_K
    n_blocks = xbuf_hbm.shape[0] // MOE_TILE

    def token_copies(r):
        return [pltpu.make_async_copy(xp_ref.at[pl.ds(r, 1), :],
                                      xbuf_hbm.at[pl.ds(dest_ref[0, 0, TOP_K * r + k], 1), :], sem)
                for k in range(TOP_K)]

    def pad_rows(e):
        first = pstart_ref[e] + cnt_ref[e]
        n_pad = (-cnt_ref[e]) & (MOE_TILE - 1)
        return first, n_pad

    def zero_copy(row):
        return pltpu.make_async_copy(zero_ref.at[pl.ds(0, 1), :], xbuf_hbm.at[pl.ds(row, 1), :], zsem)

    def zero_block_copy(b):
        return pltpu.make_async_copy(zero_ref, xbuf_hbm.at[pl.ds(b * MOE_TILE, MOE_TILE), :], zsem)

    @pl.when(step == 0)
    def _():
        zero_ref[...] = jnp.zeros_like(zero_ref)

        def start_expert(e, carry):
            first, n_pad = pad_rows(e)
            lax.fori_loop(0, n_pad, lambda r, c: (zero_copy(first + r).start(), c)[1], 0)
            return carry
        lax.fori_loop(0, N_EXPERTS, start_expert, 0)
        lax.fori_loop(n_used_ref[0], n_blocks, lambda b, c: (zero_block_copy(b).start(), c)[1], 0)

    def start(g, carry):
        for u in range(DMA_UNROLL):
            for cp in token_copies(g * DMA_UNROLL + u):
                cp.start()
        return carry
    lax.fori_loop(0, n_tok // DMA_UNROLL, start, 0)

    def wait(g, carry):
        for u in range(DMA_UNROLL):
            for cp in token_copies(g * DMA_UNROLL + u):
                cp.wait()
        return carry
    lax.fori_loop(0, n_tok // DMA_UNROLL, wait, 0)

    @pl.when(step == 0)
    def _():
        def wait_expert(e, carry):
            first, n_pad = pad_rows(e)
            lax.fori_loop(0, n_pad, lambda r, c: (zero_copy(first + r).wait(), c)[1], 0)
            return carry
        lax.fori_loop(0, N_EXPERTS, wait_expert, 0)
        lax.fori_loop(n_used_ref[0], n_blocks, lambda b, c: (zero_block_copy(b).wait(), c)[1], 0)


def _dispatch(counts, pstart, n_used, dest, xp, n_rows):
    t = xp.shape[0]
    assert MOE_TILE & (MOE_TILE - 1) == 0 and t % DISPATCH_TILE == 0 and DISPATCH_TILE % DMA_UNROLL == 0
    dest_blocks = dest.reshape(t // DISPATCH_TILE, 1, TOP_K * DISPATCH_TILE)
    grid_spec = pltpu.PrefetchScalarGridSpec(
        num_scalar_prefetch=3,
        grid=(t // DISPATCH_TILE,),
        in_specs=[
            pl.BlockSpec((1, 1, TOP_K * DISPATCH_TILE), lambda i, c, p, n: (i, 0, 0), memory_space=pltpu.SMEM),
            pl.BlockSpec((DISPATCH_TILE, PACK_W), lambda i, c, p, n: (i, 0)),
        ],
        out_specs=pl.BlockSpec(memory_space=pl.ANY),
        scratch_shapes=[
            pltpu.VMEM((MOE_TILE, PACK_W), jnp.uint32),
            pltpu.SemaphoreType.DMA(()),
            pltpu.SemaphoreType.DMA(()),
        ],
    )
    return pl.pallas_call(
        _dispatch_kernel,
        grid_spec=grid_spec,
        out_shape=jax.ShapeDtypeStruct((n_rows, PACK_W), jnp.uint32),
        compiler_params=_cparams(1),
        name="dispatch",
    )(counts, pstart, n_used, dest_blocks, xp)


def _expert_kernel(blk_e_ref, n_used_ref, first_ref, slot_ref, next_e_ref, next2_e_ref, x_ref,
                   w1_hbm, w3_hbm, w2_hbm, y_ref, w1f_ref, w3f_ref, w2f_ref, sem,
                   w1b_ref, w3b_ref, w2b_ref, *, layer):
    b = pl.program_id(0)

    def weight_copies(e, s):
        return [pltpu.make_async_copy(w_hbm.at[layer, e], wf_ref.at[s], sem.at[s])
                for w_hbm, wf_ref in ((w1_hbm, w1f_ref), (w3_hbm, w3f_ref), (w2_hbm, w2f_ref))]

    def start_weights(e, s):
        for cp in weight_copies(e, s):
            cp.start()

    @pl.when(b < n_used_ref[0])
    def _():
        @pl.when(first_ref[b] == 1)
        def _():
            s = slot_ref[b]

            @pl.when(b == 0)
            def _():
                start_weights(blk_e_ref[0], s)

                @pl.when(next_e_ref[0] >= 0)
                def _():
                    start_weights(next_e_ref[0], 1 - s)

            for cp in weight_copies(blk_e_ref[b], s):
                cp.wait()
            w1b_ref[...] = w1f_ref[s].astype(BF16)
            w3b_ref[...] = w3f_ref[s].astype(BF16)
            w2b_ref[...] = w2f_ref[s].astype(BF16)

            @pl.when(next2_e_ref[b] >= 0)
            def _():
                start_weights(next2_e_ref[b], s)

        xb = _unpack_rows(x_ref[...]).astype(BF16)
        h1 = _dot(xb, w1b_ref[...])
        h3 = _dot(xb, w3b_ref[...])
        h = (h1 * jax.nn.sigmoid(h1)) * h3
        y_ref[...] = _pack_rows(_dot(h.astype(BF16), w2b_ref[...]))

    @pl.when(b >= n_used_ref[0])
    def _():
        y_ref[...] = jnp.zeros_like(y_ref)


def _experts(xbuf, blk_e, n_used, first, slot, next_e, next2_e, layer, w1, w3, w2):
    n_blocks = xbuf.shape[0] // MOE_TILE

    def x_index(b, be, nu, *_):
        return (jnp.minimum(b, jnp.maximum(nu[0] - 1, 0)), 0)

    grid_spec = pltpu.PrefetchScalarGridSpec(
        num_scalar_prefetch=6,
        grid=(n_blocks,),
        in_specs=[
            pl.BlockSpec((MOE_TILE, PACK_W), x_index),
            pl.BlockSpec(memory_space=pl.ANY),
            pl.BlockSpec(memory_space=pl.ANY),
            pl.BlockSpec(memory_space=pl.ANY),
        ],
        out_specs=pl.BlockSpec((MOE_TILE, PACK_W), lambda b, *_: (b, 0)),
        scratch_shapes=[
            pltpu.VMEM((2, D_MODEL, EXPERT_FF), F32),
            pltpu.VMEM((2, D_MODEL, EXPERT_FF), F32),
            pltpu.VMEM((2, EXPERT_FF, D_MODEL), F32),
            pltpu.SemaphoreType.DMA((2,)),
            pltpu.VMEM((D_MODEL, EXPERT_FF), BF16),
            pltpu.VMEM((D_MODEL, EXPERT_FF), BF16),
            pltpu.VMEM((EXPERT_FF, D_MODEL), BF16),
        ],
    )
    return pl.pallas_call(
        functools.partial(_expert_kernel, layer=layer),
        grid_spec=grid_spec,
        out_shape=jax.ShapeDtypeStruct(xbuf.shape, jnp.uint32),
        compiler_params=_cparams(1),
        name="experts",
    )(blk_e, n_used, first, slot, next_e, next2_e, xbuf, w1, w3, w2)


def _combine_kernel(dest_ref, dest_next_ref, info_ref, x_ref, y_hbm, g_ref, b_ref, o_ref, rows_ref, sem):
    i = pl.program_id(0)
    n_steps = pl.num_programs(0)
    tm = x_ref.shape[0]
    slot = lax.rem(i, 2)

    def row_copies(idx_ref, s, r):
        return [pltpu.make_async_copy(y_hbm.at[pl.ds(idx_ref[0, 0, TOP_K * r + k], 1), :],
                                      rows_ref.at[s, k, pl.ds(r, 1), :], sem.at[s])
                for k in range(TOP_K)]

    def start_tile(idx_ref, s):
        def body(g, carry):
            for u in range(DMA_UNROLL):
                for cp in row_copies(idx_ref, s, g * DMA_UNROLL + u):
                    cp.start()
            return carry
        lax.fori_loop(0, tm // DMA_UNROLL, body, 0)

    @pl.when(i == 0)
    def _():
        start_tile(dest_ref, 0)

    @pl.when(i + 1 < n_steps)
    def _():
        start_tile(dest_next_ref, 1 - slot)

    def wait(g, carry):
        for u in range(DMA_UNROLL):
            for cp in row_copies(dest_ref, slot, g * DMA_UNROLL + u):
                cp.wait()
        return carry
    lax.fori_loop(0, tm // DMA_UNROLL, wait, 0)

    info = info_ref[...]
    f = None
    for k in range(TOP_K):
        term = info[:, 4 + k:5 + k] * _unpack_rows(rows_ref[slot, k])
        f = term if f is None else f + term
    y = DN_ALPHA * x_ref[...] + f
    o_ref[...] = _layer_norm(y, g_ref[...], b_ref[...])


def _combine(dest, info, x, yb, ln_g, ln_b):
    t = x.shape[0]
    n_steps = t // ROW_TILE
    assert ROW_TILE % DMA_UNROLL == 0
    dest_blocks = dest.reshape(n_steps, 1, TOP_K * ROW_TILE)
    idx_block = (1, 1, TOP_K * ROW_TILE)
    return pl.pallas_call(
        _combine_kernel,
        grid=(n_steps,),
        in_specs=[
            pl.BlockSpec(idx_block, lambda i: (i, 0, 0), memory_space=pltpu.SMEM),
            pl.BlockSpec(idx_block, lambda i: (jnp.minimum(i + 1, n_steps - 1), 0, 0), memory_space=pltpu.SMEM),
            pl.BlockSpec((ROW_TILE, LANES), lambda i: (i, 0)),
            pl.BlockSpec((ROW_TILE, D_MODEL), lambda i: (i, 0)),
            pl.BlockSpec(memory_space=pl.ANY),
            _resident((1, D_MODEL)),
            _resident((1, D_MODEL)),
        ],
        out_specs=pl.BlockSpec((ROW_TILE, D_MODEL), lambda i: (i, 0)),
        out_shape=jax.ShapeDtypeStruct((t, D_MODEL), F32),
        scratch_shapes=[
            pltpu.VMEM((2, TOP_K, ROW_TILE, PACK_W), jnp.uint32),
            pltpu.SemaphoreType.DMA((2,)),
        ],
        compiler_params=_cparams(1),
        name="combine",
    )(dest_blocks, dest_blocks, info, x, yb, ln_g, ln_b)


def _moe(x, xp, wg1, bg1, wg2, bg2, layer, w1, w3, w2, ln_g, ln_b):
    t = x.shape[0]
    w_r = jnp.concatenate([wg1, jnp.transpose(wg2, (1, 0, 2)).reshape(D_MODEL, N_EXPERTS)], axis=1)
    w_r = jnp.pad(w_r, ((0, 0), (0, LANES - w_r.shape[1])))
    b_r = jnp.pad(jnp.concatenate([bg1, bg2.reshape(-1)]), (0, LANES - N_GROUPS - N_EXPERTS))[None, :]
    info, cnt = _router(x, w_r, b_r)

    expert = info[:, 0:2].astype(jnp.int32)
    rank = info[:, 2:4].astype(jnp.int32)
    counts = cnt[0, ROUTER_LANE0:ROUTER_LANE0 + N_EXPERTS].astype(jnp.int32)
    padded = ((counts + MOE_TILE - 1) // MOE_TILE) * MOE_TILE
    pend = jnp.cumsum(padded)
    pstart = pend - padded
    experts = jnp.arange(N_EXPERTS, dtype=jnp.int32)
    dest = jnp.sum(jnp.where(expert[:, :, None] == experts, pstart, 0), axis=-1) + rank
    n_blocks = (t * TOP_K) // MOE_TILE + N_EXPERTS
    blk_start = jnp.arange(n_blocks, dtype=jnp.int32) * MOE_TILE
    blk_e = jnp.minimum(jnp.sum(pend[None, :] <= blk_start[:, None], axis=1), N_EXPERTS - 1).astype(jnp.int32)
    n_used = (pend[-1:] // MOE_TILE).astype(jnp.int32)
    first = jnp.concatenate([jnp.ones((1,), jnp.int32), (blk_e[1:] != blk_e[:-1]).astype(jnp.int32)])
    slot = (jnp.cumsum(first) - 1) % 2
    later_used = (counts[None, :] > 0) & (experts[None, :] > experts[:, None])
    next_used = jnp.min(jnp.where(later_used, experts[None, :], N_EXPERTS), axis=1)
    next2_used = jnp.concatenate([next_used, jnp.full((1,), N_EXPERTS, next_used.dtype)])[next_used]
    next_e = jnp.where(next_used < N_EXPERTS, next_used, -1)[blk_e].astype(jnp.int32)
    next2_e = jnp.where(next2_used < N_EXPERTS, next2_used, -1)[blk_e].astype(jnp.int32)

    xbuf = _dispatch(counts, pstart.astype(jnp.int32), n_used, dest, xp, n_blocks * MOE_TILE)
    yb = _experts(xbuf, blk_e, n_used, first, slot.astype(jnp.int32), next_e, next2_e, layer, w1, w3, w2)
    return _combine(dest, info, x, yb, ln_g, ln_b)


def kernel(x, mem, a_w_in, a_sgu_ln_g, a_sgu_ln_b, a_ws, a_bs, a_w_out, b_w_in, b_lambda, b_subln_g, b_w_out, shared_w_kv, rel_bias, mem_w_kv, ln_g, ln_b, moe_wg1, moe_bg1, moe_wg2, moe_bg2, moe_w1, moe_w3, moe_w2):
    b_, s_, d_ = x.shape
    assert b_ == 1 and d_ == D_MODEL
    h = x.reshape(s_, d_)
    kt_mem, v_mem = _memkv(mem.reshape(MEM_TOKENS, d_), mem_w_kv)
    bias_table = _near_bias_tables(rel_bias)
    shared_k = shared_v = None
    for l in range(DEPTH):
        if l < N_A_LAYERS:
            i = l
            mix, qm = _pre_a(h, a_w_in[i].astype(BF16), a_sgu_ln_g[i][None, :], a_sgu_ln_b[i][None, :],
                             a_ws[i], jnp.transpose(a_bs[i]))
            w_out = a_w_out[i]
        else:
            i = l - N_A_LAYERS
            qd, qm = _pre_b(h, jnp.transpose(b_w_in[i][:, :DIFF_QK_W]).astype(BF16),
                            b_w_in[i][:, DIFF_QK_W:].astype(BF16))
            lambda_init = 0.8 - 0.6 * math.exp(-0.3 * l)
            mix = _attention(qd, shared_k, shared_v, bias_table, b_lambda[i], b_subln_g[i][None, :],
                             lambda_init)
            w_out = b_w_out[i]
        h, hp = _post(mix, qm, kt_mem, v_mem, l, h, w_out.astype(BF16), ln_g[l, 0][None, :], ln_b[l, 0][None, :])
        h = _moe(h, hp, moe_wg1[l], moe_bg1[l], moe_wg2[l], moe_bg2[l], l, moe_w1, moe_w3, moe_w2,
                 ln_g[l, 1][None, :], ln_b[l, 1][None, :])
        if l == N_A_LAYERS - 1:
            shared_k, shared_v = _kvproj(h, shared_w_kv[:, :DIFF_QK_W].astype(BF16),
                                         jnp.transpose(shared_w_kv[:, DIFF_QK_W:]).astype(BF16))
    return h.reshape(b_, s_, d_)
```

```python
import functools
import math

import jax
import jax.numpy as jnp
from jax import lax
from jax.experimental import pallas as pl
from jax.experimental.pallas import tpu as pltpu

F32 = jnp.float32
BF16 = jnp.bfloat16

D_MODEL = 2048
DEPTH = 4
CHUNK = 64
N_A_LAYERS = DEPTH // 2
HEAD_DIM = 128
MEM_TOKENS = 256
MEM_HEADS = 4
MEM_W = MEM_HEADS * HEAD_DIM
MIX_W = D_MODEL - MEM_W
SGU_CHUNK = 128
SGU_GROUPS = 4
SGU_GROUP_W = MIX_W // SGU_GROUPS
DIFF_HEADS = MIX_W // (2 * HEAD_DIM)
DIFF_QK_W = 2 * DIFF_HEADS * HEAD_DIM
DIFF_V_DIM = 2 * HEAD_DIM
REL_BUCKETS = 32
REL_MAX_DIST = 128
N_GROUPS = 4
EXPERTS_PER_GROUP = 8
N_EXPERTS = N_GROUPS * EXPERTS_PER_GROUP
TOP_K = 2
EXPERT_FF = 512
DN_ALPHA = (2 * DEPTH) ** 0.25
LN_EPS = 1e-5
QK_SCALE = HEAD_DIM ** -0.5
SQRT_HALF = math.sqrt(0.5)
LOG2E = math.log2(math.e)
NEG_BIG = -1e30

LANES = 128
VMEM_LIMIT_BYTES = 56 * 1024 * 1024

ROW_TILE = 256
ROUTER_TILE = 512
ATT_TILE = 256
ATT_FAR = 2
ATT_NEAR = ATT_FAR + 1
MOE_TILE = 128
ROUTER_LANE0 = N_GROUPS


def _cparams(n_axes=1):
    return pltpu.CompilerParams(
        dimension_semantics=("arbitrary",) * n_axes,
        vmem_limit_bytes=VMEM_LIMIT_BYTES,
    )


def _dot(a, b):
    return jnp.dot(a, b, preferred_element_type=F32)


def _dot_nt(a, b):
    return lax.dot_general(a, b, (((1,), (1,)), ((), ())), preferred_element_type=F32)


def _gelu(x):
    return 0.5 * x * (1.0 + lax.erf(x * SQRT_HALF))


def _layer_norm(y, g, b):
    mu = jnp.mean(y, axis=-1, keepdims=True)
    d = y - mu
    var = jnp.mean(d * d, axis=-1, keepdims=True)
    return d * lax.rsqrt(var + LN_EPS) * g + b


HI16 = 0xFFFF0000
PACK_W = D_MODEL // 2


def _pack_rows(y):
    lo = y[:, :PACK_W].astype(BF16).astype(F32)
    hi = y[:, PACK_W:].astype(BF16).astype(F32)
    return (pltpu.bitcast(lo, jnp.uint32) >> 16) | (pltpu.bitcast(hi, jnp.uint32) & jnp.uint32(HI16))


def _unpack_rows(w):
    lo = pltpu.bitcast(w << 16, F32)
    hi = pltpu.bitcast(w & jnp.uint32(HI16), F32)
    return jnp.concatenate([lo, hi], axis=-1)


def _resident(shape):
    nd = len(shape)
    return pl.BlockSpec(shape, lambda *_: (0,) * nd, pipeline_mode=pl.Buffered(1))


def _memkv_kernel(mem_ref, w_ref, kt_ref, v_ref):
    kv = _dot(mem_ref[...].astype(BF16), w_ref[0].astype(BF16))
    kt_ref[0] = kv[:, :MEM_W].T.astype(BF16)
    v_ref[0] = kv[:, MEM_W:].astype(BF16)


def _memkv(mem, mem_w_kv):
    n_layers = mem_w_kv.shape[0]
    m = mem.shape[0]
    return pl.pallas_call(
        _memkv_kernel,
        grid=(n_layers,),
        in_specs=[
            pl.BlockSpec((m, D_MODEL), lambda l: (0, 0)),
            pl.BlockSpec((1, D_MODEL, 2 * MEM_W), lambda l: (l, 0, 0)),
        ],
        out_specs=[
            pl.BlockSpec((1, MEM_W, m), lambda l: (l, 0, 0)),
            pl.BlockSpec((1, m, MEM_W), lambda l: (l, 0, 0)),
        ],
        out_shape=[
            jax.ShapeDtypeStruct((n_layers, MEM_W, m), BF16),
            jax.ShapeDtypeStruct((n_layers, m, MEM_W), BF16),
        ],
        compiler_params=_cparams(1),
        name="memkv",
    )(mem, mem_w_kv)


def _pre_a_kernel(x_ref, w_ref, lng_ref, lnb_ref, ws_ref, bst_ref, mix_ref, qm_ref, vn_ref):
    tm = x_ref.shape[0]
    x = x_ref[...].astype(BF16)
    v = _gelu(_dot(x, w_ref[:, MIX_W:2 * MIX_W]))
    vn_ref[...] = _layer_norm(v, lng_ref[...], lnb_ref[...]).astype(BF16)
    row = lax.broadcasted_iota(jnp.int32, (SGU_CHUNK, SGU_CHUNK), 0)
    col = lax.broadcasted_iota(jnp.int32, (SGU_CHUNK, SGU_CHUNK), 1)
    shift = CHUNK.bit_length() - 1
    keep = (col >> shift) <= (row >> shift)
    for g in range(SGU_GROUPS):
        cols = slice(g * SGU_GROUP_W, (g + 1) * SGU_GROUP_W)
        w_sp = jnp.where(keep, ws_ref[g], 0.0).astype(BF16)
        u = _gelu(_dot(x, w_ref[:, cols]))
        bias = bst_ref[:, g:g + 1]
        for c in range(tm // SGU_CHUNK):
            rows = slice(c * SGU_CHUNK, (c + 1) * SGU_CHUNK)
            gate = _dot(w_sp, vn_ref[rows, cols]) + bias
            mix_ref[rows, cols] = (u[rows] * gate).astype(BF16)
    qm_ref[...] = (_dot(x, w_ref[:, 2 * MIX_W:]) * QK_SCALE).astype(BF16)


def _pre_a(x, w_in, ln_g, ln_b, ws, bs_t):
    t = x.shape[0]
    n_in = w_in.shape[1]
    return pl.pallas_call(
        _pre_a_kernel,
        grid=(t // ROW_TILE,),
        in_specs=[
            pl.BlockSpec((ROW_TILE, D_MODEL), lambda i: (i, 0)),
            _resident((D_MODEL, n_in)),
            _resident((1, MIX_W)),
            _resident((1, MIX_W)),
            _resident((SGU_GROUPS, SGU_CHUNK, SGU_CHUNK)),
            _resident((SGU_CHUNK, SGU_GROUPS)),
        ],
        out_specs=[
            pl.BlockSpec((ROW_TILE, MIX_W), lambda i: (i, 0)),
            pl.BlockSpec((ROW_TILE, MEM_W), lambda i: (i, 0)),
        ],
        out_shape=[
            jax.ShapeDtypeStruct((t, MIX_W), BF16),
            jax.ShapeDtypeStruct((t, MEM_W), BF16),
        ],
        scratch_shapes=[pltpu.VMEM((ROW_TILE, MIX_W), BF16)],
        compiler_params=_cparams(1),
        name="pre_a",
    )(x, w_in, ln_g, ln_b, ws, bs_t)


def _pre_b_kernel(x_ref, wqt_ref, wm_ref, qt_ref, qm_ref):
    x = x_ref[...].astype(BF16)
    for h in range(DIFF_HEADS):
        rows = slice(h * 2 * HEAD_DIM, (h + 1) * 2 * HEAD_DIM)
        qt_ref[h, 0] = (_dot_nt(wqt_ref[rows, :], x) * (QK_SCALE * LOG2E)).astype(BF16)
    qm_ref[...] = (_dot(x, wm_ref[...]) * QK_SCALE).astype(BF16)


def _pre_b(x, w_q_t, w_mem):
    t = x.shape[0]
    return pl.pallas_call(
        _pre_b_kernel,
        grid=(t // ATT_TILE,),
        in_specs=[
            pl.BlockSpec((ATT_TILE, D_MODEL), lambda i: (i, 0)),
            _resident((DIFF_QK_W, D_MODEL)),
            _resident((D_MODEL, MEM_W)),
        ],
        out_specs=[
            pl.BlockSpec((DIFF_HEADS, 1, 2 * HEAD_DIM, ATT_TILE), lambda i: (0, i, 0, 0)),
            pl.BlockSpec((ATT_TILE, MEM_W), lambda i: (i, 0)),
        ],
        out_shape=[
            jax.ShapeDtypeStruct((DIFF_HEADS, t // ATT_TILE, 2 * HEAD_DIM, ATT_TILE), BF16),
            jax.ShapeDtypeStruct((t, MEM_W), BF16),
        ],
        compiler_params=_cparams(1),
        name="pre_b",
    )(x, w_q_t, w_mem)


def _kvproj_kernel(x_ref, wk_ref, wvt_ref, k_ref, vt_ref):
    x = x_ref[...].astype(BF16)
    k_ref[...] = _dot(x, wk_ref[...]).astype(BF16)
    for h in range(DIFF_HEADS):
        rows = slice(h * DIFF_V_DIM, (h + 1) * DIFF_V_DIM)
        vt_ref[h, 0] = _dot_nt(wvt_ref[rows, :], x).astype(BF16)


def _kvproj(x, w_k, w_v_t):
    t = x.shape[0]
    n_v = DIFF_HEADS * DIFF_V_DIM
    return pl.pallas_call(
        _kvproj_kernel,
        grid=(t // ATT_TILE,),
        in_specs=[
            pl.BlockSpec((ATT_TILE, D_MODEL), lambda i: (i, 0)),
            _resident((D_MODEL, DIFF_QK_W)),
            _resident((n_v, D_MODEL)),
        ],
        out_specs=[
            pl.BlockSpec((ATT_TILE, DIFF_QK_W), lambda i: (i, 0)),
            pl.BlockSpec((DIFF_HEADS, 1, DIFF_V_DIM, ATT_TILE), lambda i: (0, i, 0, 0)),
        ],
        out_shape=[
            jax.ShapeDtypeStruct((t, DIFF_QK_W), BF16),
            jax.ShapeDtypeStruct((DIFF_HEADS, t // ATT_TILE, DIFF_V_DIM, ATT_TILE), BF16),
        ],
        compiler_params=_cparams(1),
        name="kvproj",
    )(x, w_k, w_v_t)


def _attn_kernel(qt_ref, k_ref, vt_ref, bias_ref, lam_ref, g_ref, o_ref, acc1_ref, acc2_ref,
                 s_ref, p_ref, *, lambda_init):
    assert ATT_FAR in (1, 2) and ATT_NEAR == ATT_FAR + 1
    i = pl.program_id(1)
    tq = qt_ref.shape[3]
    acc1_ref[...] = jnp.zeros_like(acc1_ref)
    acc2_ref[...] = jnp.zeros_like(acc2_ref)

    def scores(j0, n_blk, slot):
        start = pl.multiple_of(j0 * tq, tq)
        kblk = k_ref[pl.ds(start, n_blk * tq), :]
        for s in range(2):
            rows = slice(s * HEAD_DIM, (s + 1) * HEAD_DIM)
            s_ref[slot, s, :n_blk * tq, :] = _dot(kblk[:, rows], qt_ref[0, 0, rows, :])

    def probs(n_blk, bias, stats, slot):
        new_stats, alphas = [], []
        for s in range(2):
            m, l = stats[2 * s], stats[2 * s + 1]
            st = s_ref[slot, s, :n_blk * tq, :]
            if bias is not None:
                st = st + bias
            m_new = jnp.maximum(m, jnp.max(st, axis=0, keepdims=True))
            alpha = jnp.exp2(m - m_new)
            p = jnp.exp2(st - m_new)
            p_ref[slot, s, :n_blk * tq, :] = p.astype(BF16)
            new_stats += [m_new, alpha * l + jnp.sum(p, axis=0, keepdims=True)]
            alphas.append(alpha)
        return tuple(new_stats), tuple(alphas)

    def accumulate(j0, n_blk, alphas, slot):
        vts = [vt_ref[0, j0 + n] for n in range(n_blk)]
        for s, acc_ref in enumerate((acc1_ref, acc2_ref)):
            pv = _dot(vts[0], p_ref[slot, s, :tq, :])
            for n in range(1, n_blk):
                pv = pv + _dot(vts[n], p_ref[slot, s, n * tq:(n + 1) * tq, :])
            acc_ref[...] = alphas[s] * acc_ref[...] + pv

    neg = jnp.full((1, tq), NEG_BIG, F32)
    zero = jnp.zeros((1, tq), F32)
    stats = (neg, zero, neg, zero)
    fb = ATT_FAR
    n_far = jnp.maximum(i - 1, 0) // fb
    near_j0 = jnp.maximum(i + 1 - ATT_NEAR, 0)
    near_slot = 2
    near_bias = bias_ref[0, 0]

    def near_scores():
        scores(near_j0, ATT_NEAR, near_slot)

    def near_probs(stats):
        return probs(ATT_NEAR, near_bias, stats, near_slot)

    def only_near(stats):
        near_scores()
        stats, alphas = near_probs(stats)
        accumulate(near_j0, ATT_NEAR, alphas, near_slot)
        return stats

    def far_then_near(stats):
        scores(0, fb, 0)

        def first_two(stats):
            out = probs(fb, None, stats, 0)
            scores(fb, fb, 1)
            return out

        def first_and_near(stats):
            out = probs(fb, None, stats, 0)
            near_scores()
            return out
        carry = lax.cond(n_far >= 2, first_two, first_and_near, stats)

        def steady(t, carry, parity):
            stats, alphas = carry
            accumulate(fb * (t - 2), fb, alphas, parity)
            out = probs(fb, None, stats, 1 - parity)
            scores(fb * t, fb, parity)
            return out

        def body(u, carry):
            t = 2 + 2 * u
            return steady(t + 1, steady(t, carry, 0), 1)
        n_steady = jnp.maximum(n_far - 2, 0)
        carry = lax.fori_loop(0, n_steady // 2, body, carry)
        carry = lax.cond(lax.rem(n_steady, 2) == 1,
                         lambda c: steady(n_far - 1, c, 0), lambda c: c, carry)

        def last_far_and_near(carry):
            stats, alphas = carry
            accumulate(fb * (n_far - 2), fb, alphas, lax.rem(n_far, 2))
            out = probs(fb, None, stats, lax.rem(n_far - 1, 2))
            near_scores()
            return out
        stats, alphas = lax.cond(n_far >= 2, last_far_and_near, lambda c: c, carry)
        accumulate(fb * (n_far - 1), fb, alphas, lax.rem(n_far - 1, 2))
        stats, alphas = near_probs(stats)
        accumulate(near_j0, ATT_NEAR, alphas, near_slot)
        return stats

    m1, l1, m2, l2 = lax.cond(n_far > 0, far_then_near, only_near, stats)

    lp = lam_ref[...]
    lam = (jnp.exp(jnp.sum(lp[0:1] * lp[1:2], axis=-1, keepdims=True))
           - jnp.exp(jnp.sum(lp[2:3] * lp[3:4], axis=-1, keepdims=True)) + lambda_init)
    ot = acc1_ref[...] / l1 - lam * (acc2_ref[...] / l2)
    ms = jnp.mean(ot * ot, axis=0, keepdims=True)
    ot = ot * lax.rsqrt(ms + LN_EPS)
    o_ref[...] = (ot.T * (g_ref[...] * (1.0 - lambda_init))).astype(o_ref.dtype)


def _attention(qt, k, vt, bias, lam_params, subln_g, lambda_init):
    t = k.shape[0]
    n_blk = t // ATT_TILE
    kern = functools.partial(_attn_kernel, lambda_init=lambda_init)
    return pl.pallas_call(
        kern,
        grid=(DIFF_HEADS, n_blk),
        in_specs=[
            pl.BlockSpec((1, 1, 2 * HEAD_DIM, ATT_TILE), lambda h, i: (h, i, 0, 0)),
            pl.BlockSpec((t, 2 * HEAD_DIM), lambda h, i: (0, h)),
            pl.BlockSpec((1, n_blk, DIFF_V_DIM, ATT_TILE), lambda h, i: (h, 0, 0, 0)),
            pl.BlockSpec((1, 1, ATT_NEAR * ATT_TILE, ATT_TILE),
                         lambda h, i: (h, _near_table_variant(i), 0, 0)),
            pl.BlockSpec((4, HEAD_DIM), lambda h, i: (0, 0)),
            pl.BlockSpec((1, DIFF_V_DIM), lambda h, i: (0, 0)),
        ],
        out_specs=pl.BlockSpec((ATT_TILE, DIFF_V_DIM), lambda h, i: (i, h)),
        out_shape=jax.ShapeDtypeStruct((t, DIFF_HEADS * DIFF_V_DIM), BF16),
        scratch_shapes=[pltpu.VMEM((DIFF_V_DIM, ATT_TILE), F32),
                        pltpu.VMEM((DIFF_V_DIM, ATT_TILE), F32),
                        pltpu.VMEM((3, 2, ATT_NEAR * ATT_TILE, ATT_TILE), F32),
                        pltpu.VMEM((3, 2, ATT_NEAR * ATT_TILE, ATT_TILE), BF16)],
        compiler_params=_cparams(2),
        name="diff_attn",
    )(qt, k, vt, bias, lam_params, subln_g)


def _relative_bucket(rel):
    n = REL_BUCKETS // 2
    max_exact = n // 2
    ret = jnp.where(rel > 0, n, 0)
    a = jnp.abs(rel)
    af = jnp.maximum(a, 1).astype(jnp.float32)
    large = max_exact + (jnp.log(af / max_exact) / math.log(REL_MAX_DIST / max_exact)
                         * (n - max_exact)).astype(jnp.int32)
    large = jnp.minimum(large, n - 1)
    return ret + jnp.where(a < max_exact, a, large)


def _near_table_variant(i):
    return jnp.where(i < ATT_NEAR - 1, i, ATT_NEAR - 1 + lax.rem(i - (ATT_NEAR - 1), ATT_FAR))


def _near_bias_tables(rel_bias):
    tq = ATT_TILE
    assert tq >= REL_MAX_DIST and tq % CHUNK == 0 and ATT_NEAR == ATT_FAR + 1
    qpos = jnp.arange(tq)[:, None]
    kpos = jnp.arange(ATT_NEAR * tq)[None, :] - (ATT_NEAR - 1) * tq
    onehot = jax.nn.one_hot(_relative_bucket(kpos - qpos), REL_BUCKETS, dtype=F32)
    bias = jnp.einsum('qkb,bh->qkh', onehot, rel_bias.astype(F32),
                      precision=lax.Precision.HIGHEST)
    far = rel_bias[_relative_bucket(jnp.array(-ATT_NEAR * tq))].astype(F32)
    visible = (kpos // CHUNK) <= (qpos // CHUNK)
    table = jnp.where(visible[:, :, None], (bias - far) * LOG2E, NEG_BIG)
    base = jnp.transpose(table, (2, 1, 0))
    masked = jnp.full_like(base[:, :tq], NEG_BIG)
    variants = []
    for i in range(ATT_NEAR - 1):
        lead = ATT_NEAR - 1 - i
        variants.append(jnp.concatenate([base[:, lead * tq:]] + [masked] * lead, axis=1))
    for covered in range(ATT_FAR):
        variants.append(jnp.concatenate([masked] * covered + [base[:, covered * tq:]], axis=1))
    return jnp.stack(variants, axis=1)


def _post_kernel(mix_ref, qm_ref, kt_ref, vm_ref, x_ref, w_ref, g_ref, b_ref, o_ref, op_ref):
    heads = []
    for h in range(MEM_HEADS):
        cols = slice(h * HEAD_DIM, (h + 1) * HEAD_DIM)
        s = _dot(qm_ref[:, cols], kt_ref[0, cols, :])
        p = jnp.exp(s - jnp.max(s, axis=-1, keepdims=True))
        l = jnp.sum(p, axis=-1, keepdims=True)
        heads.append((_dot(p.astype(BF16), vm_ref[0, :, cols]) / l).astype(BF16))
    mem_out = jnp.concatenate(heads, axis=-1)
    t = _dot(mix_ref[...], w_ref[:MIX_W, :]) + _dot(mem_out, w_ref[MIX_W:, :])
    y = _layer_norm(DN_ALPHA * x_ref[...] + t, g_ref[...], b_ref[...])
    o_ref[...] = y
    op_ref[...] = _pack_rows(y)


def _post(mix, qm, kt, vm, layer, x, w_out, ln_g, ln_b):
    t = x.shape[0]
    m = kt.shape[2]
    return pl.pallas_call(
        _post_kernel,
        grid=(t // ROW_TILE,),
        in_specs=[
            pl.BlockSpec((ROW_TILE, MIX_W), lambda i: (i, 0)),
            pl.BlockSpec((ROW_TILE, MEM_W), lambda i: (i, 0)),
            pl.BlockSpec((1, MEM_W, m), lambda i: (layer, 0, 0), pipeline_mode=pl.Buffered(1)),
            pl.BlockSpec((1, m, MEM_W), lambda i: (layer, 0, 0), pipeline_mode=pl.Buffered(1)),
            pl.BlockSpec((ROW_TILE, D_MODEL), lambda i: (i, 0)),
            _resident((D_MODEL, D_MODEL)),
            _resident((1, D_MODEL)),
            _resident((1, D_MODEL)),
        ],
        out_specs=[
            pl.BlockSpec((ROW_TILE, D_MODEL), lambda i: (i, 0)),
            pl.BlockSpec((ROW_TILE, PACK_W), lambda i: (i, 0)),
        ],
        out_shape=[
            jax.ShapeDtypeStruct((t, D_MODEL), F32),
            jax.ShapeDtypeStruct((t, PACK_W), jnp.uint32),
        ],
        compiler_params=_cparams(1),
        name="post",
    )(mix, qm, kt, vm, x, w_out, ln_g, ln_b)


def _router_kernel(x_ref, w_ref, b_ref, info_ref, cnt_ref, carry_ref, wsplit_ref):
    step = pl.program_id(0)
    tm = x_ref.shape[0]

    @pl.when(step == 0)
    def _():
        carry_ref[...] = jnp.zeros_like(carry_ref)
        w = w_ref[...]
        wh = w.astype(BF16)
        wsplit_ref[:, :LANES] = wh
        wsplit_ref[:, LANES:] = (w - wh.astype(F32)).astype(BF16)

    x = x_ref[...]
    xh = x.astype(BF16)
    xl = (x - xh.astype(F32)).astype(BF16)
    both = _dot(xh, wsplit_ref[...])
    logits = both[:, :LANES] + (both[:, LANES:] + _dot(xl, wsplit_ref[:, :LANES])) + b_ref[...]
    lane = lax.broadcasted_iota(jnp.int32, (tm, LANES), 1).astype(F32)
    no_lane = float(LANES)

    def top(mask):
        val = jnp.max(jnp.where(mask, logits, -jnp.inf), axis=-1, keepdims=True)
        idx = jnp.min(jnp.where(mask & (logits == val), lane, no_lane), axis=-1, keepdims=True)
        return val, idx

    is_group = lane < float(N_GROUPS)
    g_val, g_idx = top(is_group)
    p_sel = 1.0 / jnp.sum(jnp.where(is_group, jnp.exp(logits - g_val), 0.0), axis=-1, keepdims=True)
    first = float(ROUTER_LANE0) + float(EXPERTS_PER_GROUP) * g_idx
    in_group = (lane >= first) & (lane < first + float(EXPERTS_PER_GROUP))
    v1, i1 = top(in_group)
    v2, i2 = top(in_group & (lane != i1))
    e2 = jnp.exp(v2 - v1)
    gate1 = p_sel / (1.0 + e2)
    gate2 = p_sel * e2 / (1.0 + e2)

    hit1 = lane == i1
    hit2 = lane == i2
    onehot = jnp.where(hit1 | hit2, 1.0, 0.0)
    r = lax.broadcasted_iota(jnp.int32, (tm, tm), 0)
    c = lax.broadcasted_iota(jnp.int32, (tm, tm), 1)
    strict_lower = jnp.where(c < r, 1.0, 0.0).astype(BF16)
    before = _dot(strict_lower, onehot.astype(BF16)) + carry_ref[0:1, :]
    rank1 = jnp.sum(jnp.where(hit1, before, 0.0), axis=-1, keepdims=True)
    rank2 = jnp.sum(jnp.where(hit2, before, 0.0), axis=-1, keepdims=True)
    total = carry_ref[0:1, :] + jnp.sum(onehot, axis=0, keepdims=True)
    carry_ref[...] = jnp.broadcast_to(total, carry_ref.shape)
    cnt_ref[...] = jnp.broadcast_to(total, cnt_ref.shape)

    vals = (i1 - float(ROUTER_LANE0), i2 - float(ROUTER_LANE0), rank1, rank2, gate1, gate2)
    info = jnp.zeros((tm, LANES), F32)
    for k, val in enumerate(vals):
        info = jnp.where(lane == float(k), val, info)
    info_ref[...] = info


def _router(x, w_r, b_r):
    t = x.shape[0]
    return pl.pallas_call(
        _router_kernel,
        grid=(t // ROUTER_TILE,),
        in_specs=[
            pl.BlockSpec((ROUTER_TILE, D_MODEL), lambda i: (i, 0)),
            _resident((D_MODEL, LANES)),
            _resident((1, LANES)),
        ],
        out_specs=[
            pl.BlockSpec((ROUTER_TILE, LANES), lambda i: (i, 0)),
            pl.BlockSpec((8, LANES), lambda i: (0, 0)),
        ],
        out_shape=[
            jax.ShapeDtypeStruct((t, LANES), F32),
            jax.ShapeDtypeStruct((8, LANES), F32),
        ],
        scratch_shapes=[pltpu.VMEM((8, LANES), F32), pltpu.VMEM((D_MODEL, 2 * LANES), BF16)],
        compiler_params=_cparams(1),
        name="router",
    )(x, w_r, b_r)


DISPATCH_TILE = 512
DMA_UNROLL = 8


def _dispatch_kernel(cnt_ref, pstart_ref, n_used_ref, dest_ref, xp_ref, xbuf_hbm, zero_ref, sem, zsem):
    step = pl.program_id(0)
    n_tok = dest_ref.shape[2] // TOP_K
    n_blocks = xbuf_hbm.shape[0] // MOE_TILE

    def token_copies(g, u):
        r = g * DMA_UNROLL + u
        return [pltpu.make_async_copy(xp_ref.at[g, pl.ds(u, 1), :],
                                      xbuf_hbm.at[pl.ds(dest_ref[0, 0, TOP_K * r + k], 1), :], sem)
                for k in range(TOP_K)]

    def pad_rows(e):
        first = pstart_ref[e] + cnt_ref[e]
        n_pad = (-cnt_ref[e]) & (MOE_TILE - 1)
        return first, n_pad

    def zero_copy(row):
        return pltpu.make_async_copy(zero_ref.at[pl.ds(0, 1), :], xbuf_hbm.at[pl.ds(row, 1), :], zsem)

    def zero_block_copy(b):
        return pltpu.make_async_copy(zero_ref, xbuf_hbm.at[pl.ds(b * MOE_TILE, MOE_TILE), :], zsem)

    @pl.when(step == 0)
    def _():
        zero_ref[...] = jnp.zeros_like(zero_ref)

        def start_expert(e, carry):
            first, n_pad = pad_rows(e)
            lax.fori_loop(0, n_pad, lambda r, c: (zero_copy(first + r).start(), c)[1], 0)
            return carry
        lax.fori_loop(0, N_EXPERTS, start_expert, 0)
        lax.fori_loop(n_used_ref[0], n_blocks, lambda b, c: (zero_block_copy(b).start(), c)[1], 0)

    def start(g, carry):
        for u in range(DMA_UNROLL):
            for cp in token_copies(g, u):
                cp.start()
        return carry
    lax.fori_loop(0, n_tok // DMA_UNROLL, start, 0)

    def wait(g, carry):
        for u in range(DMA_UNROLL):
            for cp in token_copies(g, u):
                cp.wait()
        return carry
    lax.fori_loop(0, n_tok // DMA_UNROLL, wait, 0)

    @pl.when(step == 0)
    def _():
        def wait_expert(e, carry):
            first, n_pad = pad_rows(e)
            lax.fori_loop(0, n_pad, lambda r, c: (zero_copy(first + r).wait(), c)[1], 0)
            return carry
        lax.fori_loop(0, N_EXPERTS, wait_expert, 0)
        lax.fori_loop(n_used_ref[0], n_blocks, lambda b, c: (zero_block_copy(b).wait(), c)[1], 0)


def _dispatch(counts, pstart, n_used, dest, xp, n_rows):
    t = xp.shape[0]
    assert MOE_TILE & (MOE_TILE - 1) == 0 and t % DISPATCH_TILE == 0 and DISPATCH_TILE % DMA_UNROLL == 0
    dest_blocks = dest.reshape(t // DISPATCH_TILE, 1, TOP_K * DISPATCH_TILE)
    grid_spec = pltpu.PrefetchScalarGridSpec(
        num_scalar_prefetch=3,
        grid=(t // DISPATCH_TILE,),
        in_specs=[
            pl.BlockSpec((1, 1, TOP_K * DISPATCH_TILE), lambda i, c, p, n: (i, 0, 0), memory_space=pltpu.SMEM),
            pl.BlockSpec((DISPATCH_TILE // DMA_UNROLL, DMA_UNROLL, PACK_W), lambda i, c, p, n: (i, 0, 0)),
        ],
        out_specs=pl.BlockSpec(memory_space=pl.ANY),
        scratch_shapes=[
            pltpu.VMEM((MOE_TILE, PACK_W), jnp.uint32),
            pltpu.SemaphoreType.DMA(()),
            pltpu.SemaphoreType.DMA(()),
        ],
    )
    return pl.pallas_call(
        _dispatch_kernel,
        grid_spec=grid_spec,
        out_shape=jax.ShapeDtypeStruct((n_rows, PACK_W), jnp.uint32),
        compiler_params=_cparams(1),
        name="dispatch",
    )(counts, pstart, n_used, dest_blocks, xp.reshape(t // DMA_UNROLL, DMA_UNROLL, PACK_W))


def _expert_kernel(blk_e_ref, n_used_ref, first_ref, slot_ref, next_e_ref, next2_e_ref, x_ref,
                   w1_hbm, w3_hbm, w2_hbm, y_ref, w1f_ref, w3f_ref, w2f_ref, sem,
                   w1b_ref, w3b_ref, w2b_ref, *, layer):
    b = pl.program_id(0)

    def weight_copies(e, s):
        return [pltpu.make_async_copy(w_hbm.at[layer, e], wf_ref.at[s], sem.at[s])
                for w_hbm, wf_ref in ((w1_hbm, w1f_ref), (w3_hbm, w3f_ref), (w2_hbm, w2f_ref))]

    def start_weights(e, s):
        for cp in weight_copies(e, s):
            cp.start()

    @pl.when(b < n_used_ref[0])
    def _():
        @pl.when(first_ref[b] == 1)
        def _():
            s = slot_ref[b]
            e = blk_e_ref[b]
            e_next = next_e_ref[e]
            e_next2 = next2_e_ref[e]

            @pl.when(b == 0)
            def _():
                start_weights(e, s)

                @pl.when(e_next >= 0)
                def _():
                    start_weights(e_next, 1 - s)

            for cp in weight_copies(e, s):
                cp.wait()
            w1b_ref[...] = w1f_ref[s].astype(BF16)
            w3b_ref[...] = w3f_ref[s].astype(BF16)
            w2b_ref[...] = w2f_ref[s].astype(BF16)

            @pl.when(e_next2 >= 0)
            def _():
                start_weights(e_next2, s)

        xb = _unpack_rows(x_ref[...]).astype(BF16)
        h1 = _dot(xb, w1b_ref[...])
        h3 = _dot(xb, w3b_ref[...])
        h = (h1 * jax.nn.sigmoid(h1)) * h3
        y_ref[...] = _pack_rows(_dot(h.astype(BF16), w2b_ref[...]))

    @pl.when(b >= n_used_ref[0])
    def _():
        y_ref[...] = jnp.zeros_like(y_ref)


def _experts(xbuf, blk_e, n_used, first, slot, next_e, next2_e, layer, w1, w3, w2):
    n_blocks = xbuf.shape[0] // MOE_TILE

    def x_index(b, be, nu, *_):
        return (jnp.minimum(b, jnp.maximum(nu[0] - 1, 0)), 0)

    grid_spec = pltpu.PrefetchScalarGridSpec(
        num_scalar_prefetch=6,
        grid=(n_blocks,),
        in_specs=[
            pl.BlockSpec((MOE_TILE, PACK_W), x_index),
            pl.BlockSpec(memory_space=pl.ANY),
            pl.BlockSpec(memory_space=pl.ANY),
            pl.BlockSpec(memory_space=pl.ANY),
        ],
        out_specs=pl.BlockSpec((MOE_TILE, PACK_W), lambda b, *_: (b, 0)),
        scratch_shapes=[
            pltpu.VMEM((2, D_MODEL, EXPERT_FF), F32),
            pltpu.VMEM((2, D_MODEL, EXPERT_FF), F32),
            pltpu.VMEM((2, EXPERT_FF, D_MODEL), F32),
            pltpu.SemaphoreType.DMA((2,)),
            pltpu.VMEM((D_MODEL, EXPERT_FF), BF16),
            pltpu.VMEM((D_MODEL, EXPERT_FF), BF16),
            pltpu.VMEM((EXPERT_FF, D_MODEL), BF16),
        ],
    )
    return pl.pallas_call(
        functools.partial(_expert_kernel, layer=layer),
        grid_spec=grid_spec,
        out_shape=jax.ShapeDtypeStruct(xbuf.shape, jnp.uint32),
        compiler_params=_cparams(1),
        name="experts",
    )(blk_e, n_used, first, slot, next_e, next2_e, xbuf, w1, w3, w2)


def _combine_kernel(dest_ref, dest_next_ref, info_ref, x_ref, y_hbm, g_ref, b_ref, o_ref, rows_ref, sem):
    i = pl.program_id(0)
    n_steps = pl.num_programs(0)
    tm = x_ref.shape[0]
    slot = lax.rem(i, 2)

    def row_copies(idx_ref, s, g, u):
        r = g * DMA_UNROLL + u
        return [pltpu.make_async_copy(y_hbm.at[pl.ds(idx_ref[0, 0, TOP_K * r + k], 1), :],
                                      rows_ref.at[s, k, g, pl.ds(u, 1), :], sem.at[s])
                for k in range(TOP_K)]

    def start_tile(idx_ref, s):
        def body(g, carry):
            for u in range(DMA_UNROLL):
                for cp in row_copies(idx_ref, s, g, u):
                    cp.start()
            return carry
        lax.fori_loop(0, tm // DMA_UNROLL, body, 0)

    @pl.when(i == 0)
    def _():
        start_tile(dest_ref, 0)

    @pl.when(i + 1 < n_steps)
    def _():
        start_tile(dest_next_ref, 1 - slot)

    def wait(g, carry):
        for u in range(DMA_UNROLL):
            for cp in row_copies(dest_ref, slot, g, u):
                cp.wait()
        return carry
    lax.fori_loop(0, tm // DMA_UNROLL, wait, 0)

    info = info_ref[...]
    f = None
    for k in range(TOP_K):
        rows = rows_ref[slot, k].reshape(tm, PACK_W)
        term = info[:, 4 + k:5 + k] * _unpack_rows(rows)
        f = term if f is None else f + term
    y = DN_ALPHA * x_ref[...] + f
    o_ref[...] = _layer_norm(y, g_ref[...], b_ref[...])


def _combine(dest, info, x, yb, ln_g, ln_b):
    t = x.shape[0]
    n_steps = t // ROW_TILE
    assert ROW_TILE % DMA_UNROLL == 0
    dest_blocks = dest.reshape(n_steps, 1, TOP_K * ROW_TILE)
    idx_block = (1, 1, TOP_K * ROW_TILE)
    return pl.pallas_call(
        _combine_kernel,
        grid=(n_steps,),
        in_specs=[
            pl.BlockSpec(idx_block, lambda i: (i, 0, 0), memory_space=pltpu.SMEM),
            pl.BlockSpec(idx_block, lambda i: (jnp.minimum(i + 1, n_steps - 1), 0, 0), memory_space=pltpu.SMEM),
            pl.BlockSpec((ROW_TILE, LANES), lambda i: (i, 0)),
            pl.BlockSpec((ROW_TILE, D_MODEL), lambda i: (i, 0)),
            pl.BlockSpec(memory_space=pl.ANY),
            _resident((1, D_MODEL)),
            _resident((1, D_MODEL)),
        ],
        out_specs=pl.BlockSpec((ROW_TILE, D_MODEL), lambda i: (i, 0)),
        out_shape=jax.ShapeDtypeStruct((t, D_MODEL), F32),
        scratch_shapes=[
            pltpu.VMEM((2, TOP_K, ROW_TILE // DMA_UNROLL, DMA_UNROLL, PACK_W), jnp.uint32),
            pltpu.SemaphoreType.DMA((2,)),
        ],
        compiler_params=_cparams(1),
        name="combine",
    )(dest_blocks, dest_blocks, info, x, yb, ln_g, ln_b)


def _moe(x, xp, wg1, bg1, wg2, bg2, layer, w1, w3, w2, ln_g, ln_b):
    t = x.shape[0]
    w_r = jnp.concatenate([wg1, jnp.transpose(wg2, (1, 0, 2)).reshape(D_MODEL, N_EXPERTS)], axis=1)
    w_r = jnp.pad(w_r, ((0, 0), (0, LANES - w_r.shape[1])))
    b_r = jnp.pad(jnp.concatenate([bg1, bg2.reshape(-1)]), (0, LANES - N_GROUPS - N_EXPERTS))[None, :]
    info, cnt = _router(x, w_r, b_r)

    expert = info[:, 0:2].astype(jnp.int32)
    rank = info[:, 2:4].astype(jnp.int32)
    counts = cnt[0, ROUTER_LANE0:ROUTER_LANE0 + N_EXPERTS].astype(jnp.int32)
    padded = ((counts + MOE_TILE - 1) // MOE_TILE) * MOE_TILE
    pend = jnp.cumsum(padded)
    pstart = pend - padded
    experts = jnp.arange(N_EXPERTS, dtype=jnp.int32)
    dest = jnp.sum(jnp.where(expert[:, :, None] == experts, pstart, 0), axis=-1) + rank
    n_blocks = (t * TOP_K) // MOE_TILE + N_EXPERTS
    blk_start = jnp.arange(n_blocks, dtype=jnp.int32) * MOE_TILE
    blk_e = jnp.minimum(jnp.sum(pend[None, :] <= blk_start[:, None], axis=1), N_EXPERTS - 1).astype(jnp.int32)
    n_used = (pend[-1:] // MOE_TILE).astype(jnp.int32)
    first = jnp.concatenate([jnp.ones((1,), jnp.int32), (blk_e[1:] != blk_e[:-1]).astype(jnp.int32)])
    slot = (jnp.cumsum(first) - 1) % 2
    later_used = (counts[None, :] > 0) & (experts[None, :] > experts[:, None])
    later = jnp.where(later_used, experts[None, :], N_EXPERTS)
    next_used = jnp.min(later, axis=1)
    next2_used = jnp.min(jnp.where(later > next_used[:, None], later, N_EXPERTS), axis=1)
    next_e = jnp.where(next_used < N_EXPERTS, next_used, -1).astype(jnp.int32)
    next2_e = jnp.where(next2_used < N_EXPERTS, next2_used, -1).astype(jnp.int32)

    xbuf = _dispatch(counts, pstart.astype(jnp.int32), n_used, dest, xp, n_blocks * MOE_TILE)
    yb = _experts(xbuf, blk_e, n_used, first, slot.astype(jnp.int32), next_e, next2_e, layer, w1, w3, w2)
    return _combine(dest, info, x, yb, ln_g, ln_b)


def kernel(x, mem, a_w_in, a_sgu_ln_g, a_sgu_ln_b, a_ws, a_bs, a_w_out, b_w_in, b_lambda, b_subln_g, b_w_out, shared_w_kv, rel_bias, mem_w_kv, ln_g, ln_b, moe_wg1, moe_bg1, moe_wg2, moe_bg2, moe_w1, moe_w3, moe_w2):
    b_, s_, d_ = x.shape
    assert b_ == 1 and d_ == D_MODEL
    h = x.reshape(s_, d_)
    kt_mem, v_mem = _memkv(mem.reshape(MEM_TOKENS, d_), mem_w_kv)
    bias_table = _near_bias_tables(rel_bias)
    shared_k = shared_v = None
    for l in range(DEPTH):
        if l < N_A_LAYERS:
            i = l
            mix, qm = _pre_a(h, a_w_in[i].astype(BF16), a_sgu_ln_g[i][None, :], a_sgu_ln_b[i][None, :],
                             a_ws[i], jnp.transpose(a_bs[i]))
            w_out = a_w_out[i]
        else:
            i = l - N_A_LAYERS
            qd, qm = _pre_b(h, jnp.transpose(b_w_in[i][:, :DIFF_QK_W]).astype(BF16),
                            b_w_in[i][:, DIFF_QK_W:].astype(BF16))
            lambda_init = 0.8 - 0.6 * math.exp(-0.3 * l)
            mix = _attention(qd, shared_k, shared_v, bias_table, b_lambda[i], b_subln_g[i][None, :],
                             lambda_init)
            w_out = b_w_out[i]
        h, hp = _post(mix, qm, kt_mem, v_mem, l, h, w_out.astype(BF16), ln_g[l, 0][None, :], ln_b[l, 0][None, :])
        h = _moe(h, hp, moe_wg1[l], moe_bg1[l], moe_wg2[l], moe_bg2[l], l, moe_w1, moe_w3, moe_w2,
                 ln_g[l, 1][None, :], ln_b[l, 1][None, :])
        if l == N_A_LAYERS - 1:
            shared_k, shared_v = _kvproj(h, shared_w_kv[:, :DIFF_QK_W].astype(BF16),
                                         jnp.transpose(shared_w_kv[:, DIFF_QK_W:]).astype(BF16))
    return h.reshape(b_, s_, d_)
```

```python
import functools
import math

import jax
import jax.numpy as jnp
from jax import lax
from jax.experimental import pallas as pl
from jax.experimental.pallas import tpu as pltpu

F32 = jnp.float32
BF16 = jnp.bfloat16

D_MODEL = 2048
DEPTH = 4
CHUNK = 64
N_A_LAYERS = DEPTH // 2
HEAD_DIM = 128
MEM_TOKENS = 256
MEM_HEADS = 4
MEM_W = MEM_HEADS * HEAD_DIM
MIX_W = D_MODEL - MEM_W
SGU_CHUNK = 128
SGU_GROUPS = 4
SGU_GROUP_W = MIX_W // SGU_GROUPS
DIFF_HEADS = MIX_W // (2 * HEAD_DIM)
DIFF_QK_W = 2 * DIFF_HEADS * HEAD_DIM
DIFF_V_DIM = 2 * HEAD_DIM
REL_BUCKETS = 32
REL_MAX_DIST = 128
N_GROUPS = 4
EXPERTS_PER_GROUP = 8
N_EXPERTS = N_GROUPS * EXPERTS_PER_GROUP
TOP_K = 2
EXPERT_FF = 512
DN_ALPHA = (2 * DEPTH) ** 0.25
LN_EPS = 1e-5
QK_SCALE = HEAD_DIM ** -0.5
SQRT_HALF = math.sqrt(0.5)
LOG2E = math.log2(math.e)
NEG_BIG = -1e30

LANES = 128
VMEM_LIMIT_BYTES = 56 * 1024 * 1024

ROW_TILE = 256
ROUTER_TILE = 512
ATT_TILE = 256
ATT_FAR = 2
ATT_NEAR = ATT_FAR + 1
MOE_TILE = 128
ROUTER_LANE0 = N_GROUPS


def _cparams(n_axes=1):
    return pltpu.CompilerParams(
        dimension_semantics=("arbitrary",) * n_axes,
        vmem_limit_bytes=VMEM_LIMIT_BYTES,
    )


def _dot(a, b):
    return jnp.dot(a, b, preferred_element_type=F32)


def _dot_nt(a, b):
    return lax.dot_general(a, b, (((1,), (1,)), ((), ())), preferred_element_type=F32)


def _gelu(x):
    return 0.5 * x * (1.0 + lax.erf(x * SQRT_HALF))


def _layer_norm(y, g, b):
    mu = jnp.mean(y, axis=-1, keepdims=True)
    d = y - mu
    var = jnp.mean(d * d, axis=-1, keepdims=True)
    return d * lax.rsqrt(var + LN_EPS) * g + b


HI16 = 0xFFFF0000
PACK_W = D_MODEL // 2


def _pack_rows(y):
    lo = y[:, :PACK_W].astype(BF16).astype(F32)
    hi = y[:, PACK_W:].astype(BF16).astype(F32)
    return (pltpu.bitcast(lo, jnp.uint32) >> 16) | (pltpu.bitcast(hi, jnp.uint32) & jnp.uint32(HI16))


def _unpack_rows(w):
    lo = pltpu.bitcast(w << 16, F32)
    hi = pltpu.bitcast(w & jnp.uint32(HI16), F32)
    return jnp.concatenate([lo, hi], axis=-1)


def _resident(shape):
    nd = len(shape)
    return pl.BlockSpec(shape, lambda *_: (0,) * nd, pipeline_mode=pl.Buffered(1))


def _memkv_kernel(mem_ref, w_ref, kt_ref, v_ref):
    kv = _dot(mem_ref[...].astype(BF16), w_ref[0].astype(BF16))
    kt_ref[0] = kv[:, :MEM_W].T.astype(BF16)
    v_ref[0] = kv[:, MEM_W:].astype(BF16)


def _memkv(mem, mem_w_kv):
    n_layers = mem_w_kv.shape[0]
    m = mem.shape[0]
    return pl.pallas_call(
        _memkv_kernel,
        grid=(n_layers,),
        in_specs=[
            pl.BlockSpec((m, D_MODEL), lambda l: (0, 0)),
            pl.BlockSpec((1, D_MODEL, 2 * MEM_W), lambda l: (l, 0, 0)),
        ],
        out_specs=[
            pl.BlockSpec((1, MEM_W, m), lambda l: (l, 0, 0)),
            pl.BlockSpec((1, m, MEM_W), lambda l: (l, 0, 0)),
        ],
        out_shape=[
            jax.ShapeDtypeStruct((n_layers, MEM_W, m), BF16),
            jax.ShapeDtypeStruct((n_layers, m, MEM_W), BF16),
        ],
        compiler_params=_cparams(1),
        name="memkv",
    )(mem, mem_w_kv)


def _pre_a_kernel(x_ref, w_ref, lng_ref, lnb_ref, ws_ref, bst_ref, mix_ref, qm_ref, vn_ref):
    tm = x_ref.shape[0]
    x = x_ref[...].astype(BF16)
    v = _gelu(_dot(x, w_ref[:, MIX_W:2 * MIX_W]))
    vn_ref[...] = _layer_norm(v, lng_ref[...], lnb_ref[...]).astype(BF16)
    row = lax.broadcasted_iota(jnp.int32, (SGU_CHUNK, SGU_CHUNK), 0)
    col = lax.broadcasted_iota(jnp.int32, (SGU_CHUNK, SGU_CHUNK), 1)
    shift = CHUNK.bit_length() - 1
    keep = (col >> shift) <= (row >> shift)
    for g in range(SGU_GROUPS):
        cols = slice(g * SGU_GROUP_W, (g + 1) * SGU_GROUP_W)
        w_sp = jnp.where(keep, ws_ref[g], 0.0).astype(BF16)
        u = _gelu(_dot(x, w_ref[:, cols]))
        bias = bst_ref[:, g:g + 1]
        for c in range(tm // SGU_CHUNK):
            rows = slice(c * SGU_CHUNK, (c + 1) * SGU_CHUNK)
            gate = _dot(w_sp, vn_ref[rows, cols]) + bias
            mix_ref[rows, cols] = (u[rows] * gate).astype(BF16)
    qm_ref[...] = (_dot(x, w_ref[:, 2 * MIX_W:]) * QK_SCALE).astype(BF16)


def _pre_a(x, w_in, ln_g, ln_b, ws, bs_t):
    t = x.shape[0]
    n_in = w_in.shape[1]
    return pl.pallas_call(
        _pre_a_kernel,
        grid=(t // ROW_TILE,),
        in_specs=[
            pl.BlockSpec((ROW_TILE, D_MODEL), lambda i: (i, 0)),
            _resident((D_MODEL, n_in)),
            _resident((1, MIX_W)),
            _resident((1, MIX_W)),
            _resident((SGU_GROUPS, SGU_CHUNK, SGU_CHUNK)),
            _resident((SGU_CHUNK, SGU_GROUPS)),
        ],
        out_specs=[
            pl.BlockSpec((ROW_TILE, MIX_W), lambda i: (i, 0)),
            pl.BlockSpec((ROW_TILE, MEM_W), lambda i: (i, 0)),
        ],
        out_shape=[
            jax.ShapeDtypeStruct((t, MIX_W), BF16),
            jax.ShapeDtypeStruct((t, MEM_W), BF16),
        ],
        scratch_shapes=[pltpu.VMEM((ROW_TILE, MIX_W), BF16)],
        compiler_params=_cparams(1),
        name="pre_a",
    )(x, w_in, ln_g, ln_b, ws, bs_t)


def _pre_b_kernel(x_ref, wqt_ref, wm_ref, qt_ref, qm_ref):
    x = x_ref[...].astype(BF16)
    for h in range(DIFF_HEADS):
        rows = slice(h * 2 * HEAD_DIM, (h + 1) * 2 * HEAD_DIM)
        qt_ref[h, 0] = (_dot_nt(wqt_ref[rows, :], x) * (QK_SCALE * LOG2E)).astype(BF16)
    qm_ref[...] = (_dot(x, wm_ref[...]) * QK_SCALE).astype(BF16)


def _pre_b(x, w_q_t, w_mem):
    t = x.shape[0]
    return pl.pallas_call(
        _pre_b_kernel,
        grid=(t // ATT_TILE,),
        in_specs=[
            pl.BlockSpec((ATT_TILE, D_MODEL), lambda i: (i, 0)),
            _resident((DIFF_QK_W, D_MODEL)),
            _resident((D_MODEL, MEM_W)),
        ],
        out_specs=[
            pl.BlockSpec((DIFF_HEADS, 1, 2 * HEAD_DIM, ATT_TILE), lambda i: (0, i, 0, 0)),
            pl.BlockSpec((ATT_TILE, MEM_W), lambda i: (i, 0)),
        ],
        out_shape=[
            jax.ShapeDtypeStruct((DIFF_HEADS, t // ATT_TILE, 2 * HEAD_DIM, ATT_TILE), BF16),
            jax.ShapeDtypeStruct((t, MEM_W), BF16),
        ],
        compiler_params=_cparams(1),
        name="pre_b",
    )(x, w_q_t, w_mem)


def _kvproj_kernel(x_ref, wk_ref, wvt_ref, k_ref, vt_ref):
    x = x_ref[...].astype(BF16)
    k_ref[...] = _dot(x, wk_ref[...]).astype(BF16)
    for h in range(DIFF_HEADS):
        rows = slice(h * DIFF_V_DIM, (h + 1) * DIFF_V_DIM)
        vt_ref[h, 0] = _dot_nt(wvt_ref[rows, :], x).astype(BF16)


def _kvproj(x, w_k, w_v_t):
    t = x.shape[0]
    n_v = DIFF_HEADS * DIFF_V_DIM
    return pl.pallas_call(
        _kvproj_kernel,
        grid=(t // ATT_TILE,),
        in_specs=[
            pl.BlockSpec((ATT_TILE, D_MODEL), lambda i: (i, 0)),
            _resident((D_MODEL, DIFF_QK_W)),
            _resident((n_v, D_MODEL)),
        ],
        out_specs=[
            pl.BlockSpec((ATT_TILE, DIFF_QK_W), lambda i: (i, 0)),
            pl.BlockSpec((DIFF_HEADS, 1, DIFF_V_DIM, ATT_TILE), lambda i: (0, i, 0, 0)),
        ],
        out_shape=[
            jax.ShapeDtypeStruct((t, DIFF_QK_W), BF16),
            jax.ShapeDtypeStruct((DIFF_HEADS, t // ATT_TILE, DIFF_V_DIM, ATT_TILE), BF16),
        ],
        compiler_params=_cparams(1),
        name="kvproj",
    )(x, w_k, w_v_t)


def _attn_kernel(qt_ref, k_ref, vt_ref, bias_ref, lam_ref, g_ref, o_ref, acc1_ref, acc2_ref,
                 s_ref, p_ref, *, lambda_init):
    assert ATT_FAR in (1, 2) and ATT_NEAR == ATT_FAR + 1
    i = pl.program_id(1)
    tq = qt_ref.shape[3]
    acc1_ref[...] = jnp.zeros_like(acc1_ref)
    acc2_ref[...] = jnp.zeros_like(acc2_ref)

    def scores(j0, n_blk, slot):
        start = pl.multiple_of(j0 * tq, tq)
        kblk = k_ref[pl.ds(start, n_blk * tq), :]
        for s in range(2):
            rows = slice(s * HEAD_DIM, (s + 1) * HEAD_DIM)
            s_ref[slot, s, :n_blk * tq, :] = _dot(kblk[:, rows], qt_ref[0, 0, rows, :])

    def probs(n_blk, bias, stats, slot):
        new_stats, alphas = [], []
        for s in range(2):
            m, l = stats[2 * s], stats[2 * s + 1]
            st = s_ref[slot, s, :n_blk * tq, :]
            if bias is not None:
                st = st + bias
            m_new = jnp.maximum(m, jnp.max(st, axis=0, keepdims=True))
            alpha = jnp.exp2(m - m_new)
            p = jnp.exp2(st - m_new)
            p_ref[slot, s, :n_blk * tq, :] = p.astype(BF16)
            new_stats += [m_new, alpha * l + jnp.sum(p, axis=0, keepdims=True)]
            alphas.append(alpha)
        return tuple(new_stats), tuple(alphas)

    def accumulate(j0, n_blk, alphas, slot):
        vts = [vt_ref[0, j0 + n] for n in range(n_blk)]
        for s, acc_ref in enumerate((acc1_ref, acc2_ref)):
            pv = _dot(vts[0], p_ref[slot, s, :tq, :])
            for n in range(1, n_blk):
                pv = pv + _dot(vts[n], p_ref[slot, s, n * tq:(n + 1) * tq, :])
            acc_ref[...] = alphas[s] * acc_ref[...] + pv

    neg = jnp.full((1, tq), NEG_BIG, F32)
    zero = jnp.zeros((1, tq), F32)
    stats = (neg, zero, neg, zero)
    fb = ATT_FAR
    n_far = jnp.maximum(i - 1, 0) // fb
    near_j0 = jnp.maximum(i + 1 - ATT_NEAR, 0)
    near_slot = 2
    near_bias = bias_ref[0, 0]

    def near_scores():
        scores(near_j0, ATT_NEAR, near_slot)

    def near_probs(stats):
        return probs(ATT_NEAR, near_bias, stats, near_slot)

    def only_near(stats):
        near_scores()
        stats, alphas = near_probs(stats)
        accumulate(near_j0, ATT_NEAR, alphas, near_slot)
        return stats

    def far_then_near(stats):
        scores(0, fb, 0)

        def first_two(stats):
            out = probs(fb, None, stats, 0)
            scores(fb, fb, 1)
            return out

        def first_and_near(stats):
            out = probs(fb, None, stats, 0)
            near_scores()
            return out
        carry = lax.cond(n_far >= 2, first_two, first_and_near, stats)

        def steady(t, carry, parity):
            stats, alphas = carry
            accumulate(fb * (t - 2), fb, alphas, parity)
            out = probs(fb, None, stats, 1 - parity)
            scores(fb * t, fb, parity)
            return out

        def body(u, carry):
            t = 2 + 2 * u
            return steady(t + 1, steady(t, carry, 0), 1)
        n_steady = jnp.maximum(n_far - 2, 0)
        carry = lax.fori_loop(0, n_steady // 2, body, carry)
        carry = lax.cond(lax.rem(n_steady, 2) == 1,
                         lambda c: steady(n_far - 1, c, 0), lambda c: c, carry)

        def finish(carry, parity):
            def last_far_and_near(carry):
                stats, alphas = carry
                accumulate(fb * (n_far - 2), fb, alphas, parity)
                out = probs(fb, None, stats, 1 - parity)
                near_scores()
                return out
            stats, alphas = lax.cond(n_far >= 2, last_far_and_near, lambda c: c, carry)
            accumulate(fb * (n_far - 1), fb, alphas, 1 - parity)
            stats, alphas = near_probs(stats)
            accumulate(near_j0, ATT_NEAR, alphas, near_slot)
            return stats

        return lax.cond(lax.rem(n_far, 2) == 0, lambda c: finish(c, 0), lambda c: finish(c, 1), carry)

    m1, l1, m2, l2 = lax.cond(n_far > 0, far_then_near, only_near, stats)

    lp = lam_ref[...]
    lam = (jnp.exp(jnp.sum(lp[0:1] * lp[1:2], axis=-1, keepdims=True))
           - jnp.exp(jnp.sum(lp[2:3] * lp[3:4], axis=-1, keepdims=True)) + lambda_init)
    ot = acc1_ref[...] / l1 - lam * (acc2_ref[...] / l2)
    ms = jnp.mean(ot * ot, axis=0, keepdims=True)
    ot = ot * lax.rsqrt(ms + LN_EPS)
    o_ref[...] = (ot.T * (g_ref[...] * (1.0 - lambda_init))).astype(o_ref.dtype)


def _attention(qt, k, vt, bias, lam_params, subln_g, lambda_init):
    t = k.shape[0]
    n_blk = t // ATT_TILE
    kern = functools.partial(_attn_kernel, lambda_init=lambda_init)
    return pl.pallas_call(
        kern,
        grid=(DIFF_HEADS, n_blk),
        in_specs=[
            pl.BlockSpec((1, 1, 2 * HEAD_DIM, ATT_TILE), lambda h, i: (h, i, 0, 0)),
            pl.BlockSpec((t, 2 * HEAD_DIM), lambda h, i: (0, h)),
            pl.BlockSpec((1, n_blk, DIFF_V_DIM, ATT_TILE), lambda h, i: (h, 0, 0, 0)),
            pl.BlockSpec((1, 1, ATT_NEAR * ATT_TILE, ATT_TILE),
                         lambda h, i: (h, _near_table_variant(i), 0, 0)),
            pl.BlockSpec((4, HEAD_DIM), lambda h, i: (0, 0)),
            pl.BlockSpec((1, DIFF_V_DIM), lambda h, i: (0, 0)),
        ],
        out_specs=pl.BlockSpec((ATT_TILE, DIFF_V_DIM), lambda h, i: (i, h)),
        out_shape=jax.ShapeDtypeStruct((t, DIFF_HEADS * DIFF_V_DIM), BF16),
        scratch_shapes=[pltpu.VMEM((DIFF_V_DIM, ATT_TILE), F32),
                        pltpu.VMEM((DIFF_V_DIM, ATT_TILE), F32),
                        pltpu.VMEM((3, 2, ATT_NEAR * ATT_TILE, ATT_TILE), F32),
                        pltpu.VMEM((3, 2, ATT_NEAR * ATT_TILE, ATT_TILE), BF16)],
        compiler_params=_cparams(2),
        name="diff_attn",
    )(qt, k, vt, bias, lam_params, subln_g)


def _relative_bucket(rel):
    n = REL_BUCKETS // 2
    max_exact = n // 2
    ret = jnp.where(rel > 0, n, 0)
    a = jnp.abs(rel)
    af = jnp.maximum(a, 1).astype(jnp.float32)
    large = max_exact + (jnp.log(af / max_exact) / math.log(REL_MAX_DIST / max_exact)
                         * (n - max_exact)).astype(jnp.int32)
    large = jnp.minimum(large, n - 1)
    return ret + jnp.where(a < max_exact, a, large)


def _near_table_variant(i):
    return jnp.where(i < ATT_NEAR - 1, i, ATT_NEAR - 1 + lax.rem(i - (ATT_NEAR - 1), ATT_FAR))


def _near_bias_tables(rel_bias):
    tq = ATT_TILE
    assert tq >= REL_MAX_DIST and tq % CHUNK == 0 and ATT_NEAR == ATT_FAR + 1
    qpos = jnp.arange(tq)[:, None]
    kpos = jnp.arange(ATT_NEAR * tq)[None, :] - (ATT_NEAR - 1) * tq
    onehot = jax.nn.one_hot(_relative_bucket(kpos - qpos), REL_BUCKETS, dtype=F32)
    bias = jnp.einsum('qkb,bh->qkh', onehot, rel_bias.astype(F32),
                      precision=lax.Precision.HIGHEST)
    far = rel_bias[_relative_bucket(jnp.array(-ATT_NEAR * tq))].astype(F32)
    visible = (kpos // CHUNK) <= (qpos // CHUNK)
    table = jnp.where(visible[:, :, None], (bias - far) * LOG2E, NEG_BIG)
    base = jnp.transpose(table, (2, 1, 0))
    masked = jnp.full_like(base[:, :tq], NEG_BIG)
    variants = []
    for i in range(ATT_NEAR - 1):
        lead = ATT_NEAR - 1 - i
        variants.append(jnp.concatenate([base[:, lead * tq:]] + [masked] * lead, axis=1))
    for covered in range(ATT_FAR):
        variants.append(jnp.concatenate([masked] * covered + [base[:, covered * tq:]], axis=1))
    return jnp.stack(variants, axis=1)


def _post_kernel(mix_ref, qm_ref, kt_ref, vm_ref, x_ref, w_ref, g_ref, b_ref, o_ref, op_ref):
    heads = []
    for h in range(MEM_HEADS):
        cols = slice(h * HEAD_DIM, (h + 1) * HEAD_DIM)
        s = _dot(qm_ref[:, cols], kt_ref[0, cols, :])
        p = jnp.exp(s - jnp.max(s, axis=-1, keepdims=True))
        l = jnp.sum(p, axis=-1, keepdims=True)
        heads.append((_dot(p.astype(BF16), vm_ref[0, :, cols]) / l).astype(BF16))
    mem_out = jnp.concatenate(heads, axis=-1)
    t = _dot(mix_ref[...], w_ref[:MIX_W, :]) + _dot(mem_out, w_ref[MIX_W:, :])
    y = _layer_norm(DN_ALPHA * x_ref[...] + t, g_ref[...], b_ref[...])
    o_ref[...] = y
    op_ref[...] = _pack_rows(y)


def _post(mix, qm, kt, vm, layer, x, w_out, ln_g, ln_b):
    t = x.shape[0]
    m = kt.shape[2]
    return pl.pallas_call(
        _post_kernel,
        grid=(t // ROW_TILE,),
        in_specs=[
            pl.BlockSpec((ROW_TILE, MIX_W), lambda i: (i, 0)),
            pl.BlockSpec((ROW_TILE, MEM_W), lambda i: (i, 0)),
            pl.BlockSpec((1, MEM_W, m), lambda i: (layer, 0, 0), pipeline_mode=pl.Buffered(1)),
            pl.BlockSpec((1, m, MEM_W), lambda i: (layer, 0, 0), pipeline_mode=pl.Buffered(1)),
            pl.BlockSpec((ROW_TILE, D_MODEL), lambda i: (i, 0)),
            _resident((D_MODEL, D_MODEL)),
            _resident((1, D_MODEL)),
            _resident((1, D_MODEL)),
        ],
        out_specs=[
            pl.BlockSpec((ROW_TILE, D_MODEL), lambda i: (i, 0)),
            pl.BlockSpec((ROW_TILE, PACK_W), lambda i: (i, 0)),
        ],
        out_shape=[
            jax.ShapeDtypeStruct((t, D_MODEL), F32),
            jax.ShapeDtypeStruct((t, PACK_W), jnp.uint32),
        ],
        compiler_params=_cparams(1),
        name="post",
    )(mix, qm, kt, vm, x, w_out, ln_g, ln_b)


def _router_kernel(x_ref, w_ref, b_ref, info_ref, cnt_ref, carry_ref, wsplit_ref):
    step = pl.program_id(0)
    tm = x_ref.shape[0]

    @pl.when(step == 0)
    def _():
        carry_ref[...] = jnp.zeros_like(carry_ref)
        w = w_ref[...]
        wh = w.astype(BF16)
        wsplit_ref[:, :LANES] = wh
        wsplit_ref[:, LANES:] = (w - wh.astype(F32)).astype(BF16)

    x = x_ref[...]
    xh = x.astype(BF16)
    xl = (x - xh.astype(F32)).astype(BF16)
    both = _dot(xh, wsplit_ref[...])
    logits = both[:, :LANES] + (both[:, LANES:] + _dot(xl, wsplit_ref[:, :LANES])) + b_ref[...]
    lane = lax.broadcasted_iota(jnp.int32, (tm, LANES), 1).astype(F32)
    no_lane = float(LANES)

    def top(mask):
        val = jnp.max(jnp.where(mask, logits, -jnp.inf), axis=-1, keepdims=True)
        idx = jnp.min(jnp.where(mask & (logits == val), lane, no_lane), axis=-1, keepdims=True)
        return val, idx

    is_group = lane < float(N_GROUPS)
    g_val, g_idx = top(is_group)
    p_sel = 1.0 / jnp.sum(jnp.where(is_group, jnp.exp(logits - g_val), 0.0), axis=-1, keepdims=True)
    first = float(ROUTER_LANE0) + float(EXPERTS_PER_GROUP) * g_idx
    in_group = (lane >= first) & (lane < first + float(EXPERTS_PER_GROUP))
    v1, i1 = top(in_group)
    v2, i2 = top(in_group & (lane != i1))
    e2 = jnp.exp(v2 - v1)
    gate1 = p_sel / (1.0 + e2)
    gate2 = p_sel * e2 / (1.0 + e2)

    hit1 = lane == i1
    hit2 = lane == i2
    onehot = jnp.where(hit1 | hit2, 1.0, 0.0)
    r = lax.broadcasted_iota(jnp.int32, (tm, tm), 0)
    c = lax.broadcasted_iota(jnp.int32, (tm, tm), 1)
    strict_lower = jnp.where(c < r, 1.0, 0.0).astype(BF16)
    before = _dot(strict_lower, onehot.astype(BF16)) + carry_ref[0:1, :]
    rank1 = jnp.sum(jnp.where(hit1, before, 0.0), axis=-1, keepdims=True)
    rank2 = jnp.sum(jnp.where(hit2, before, 0.0), axis=-1, keepdims=True)
    total = carry_ref[0:1, :] + jnp.sum(onehot, axis=0, keepdims=True)
    carry_ref[...] = jnp.broadcast_to(total, carry_ref.shape)
    cnt_ref[...] = jnp.broadcast_to(total, cnt_ref.shape)

    vals = (i1 - float(ROUTER_LANE0), i2 - float(ROUTER_LANE0), rank1, rank2, gate1, gate2)
    info = jnp.zeros((tm, LANES), F32)
    for k, val in enumerate(vals):
        info = jnp.where(lane == float(k), val, info)
    info_ref[...] = info


def _router(x, w_r, b_r):
    t = x.shape[0]
    return pl.pallas_call(
        _router_kernel,
        grid=(t // ROUTER_TILE,),
        in_specs=[
            pl.BlockSpec((ROUTER_TILE, D_MODEL), lambda i: (i, 0)),
            _resident((D_MODEL, LANES)),
            _resident((1, LANES)),
        ],
        out_specs=[
            pl.BlockSpec((ROUTER_TILE, LANES), lambda i: (i, 0)),
            pl.BlockSpec((8, LANES), lambda i: (0, 0)),
        ],
        out_shape=[
            jax.ShapeDtypeStruct((t, LANES), F32),
            jax.ShapeDtypeStruct((8, LANES), F32),
        ],
        scratch_shapes=[pltpu.VMEM((8, LANES), F32), pltpu.VMEM((D_MODEL, 2 * LANES), BF16)],
        compiler_params=_cparams(1),
        name="router",
    )(x, w_r, b_r)


DISPATCH_TILE = 512
DMA_UNROLL = 8


def _dispatch_kernel(cnt_ref, pstart_ref, n_used_ref, dest_ref, xp_ref, xbuf_hbm, zero_ref, sem, zsem):
    step = pl.program_id(0)
    n_tok = dest_ref.shape[2] // TOP_K
    n_blocks = xbuf_hbm.shape[0] // MOE_TILE

    def token_copies(g, u):
        r = g * DMA_UNROLL + u
        return [pltpu.make_async_copy(xp_ref.at[g, pl.ds(u, 1), :],
                                      xbuf_hbm.at[pl.ds(dest_ref[0, 0, TOP_K * r + k], 1), :], sem)
                for k in range(TOP_K)]

    def pad_rows(e):
        first = pstart_ref[e] + cnt_ref[e]
        n_pad = (-cnt_ref[e]) & (MOE_TILE - 1)
        return first, n_pad

    def zero_copy(row):
        return pltpu.make_async_copy(zero_ref.at[pl.ds(0, 1), :], xbuf_hbm.at[pl.ds(row, 1), :], zsem)

    def zero_block_copy(b):
        return pltpu.make_async_copy(zero_ref, xbuf_hbm.at[pl.ds(b * MOE_TILE, MOE_TILE), :], zsem)

    @pl.when(step == 0)
    def _():
        zero_ref[...] = jnp.zeros_like(zero_ref)

        def start_expert(e, carry):
            first, n_pad = pad_rows(e)
            lax.fori_loop(0, n_pad, lambda r, c: (zero_copy(first + r).start(), c)[1], 0)
            return carry
        lax.fori_loop(0, N_EXPERTS, start_expert, 0)
        lax.fori_loop(n_used_ref[0], n_blocks, lambda b, c: (zero_block_copy(b).start(), c)[1], 0)

    def start(g, carry):
        for u in range(DMA_UNROLL):
            for cp in token_copies(g, u):
                cp.start()
        return carry
    lax.fori_loop(0, n_tok // DMA_UNROLL, start, 0)

    def wait(g, carry):
        for u in range(DMA_UNROLL):
            for cp in token_copies(g, u):
                cp.wait()
        return carry
    lax.fori_loop(0, n_tok // DMA_UNROLL, wait, 0)

    @pl.when(step == 0)
    def _():
        def wait_expert(e, carry):
            first, n_pad = pad_rows(e)
            lax.fori_loop(0, n_pad, lambda r, c: (zero_copy(first + r).wait(), c)[1], 0)
            return carry
        lax.fori_loop(0, N_EXPERTS, wait_expert, 0)
        lax.fori_loop(n_used_ref[0], n_blocks, lambda b, c: (zero_block_copy(b).wait(), c)[1], 0)


def _dispatch(counts, pstart, n_used, dest, xp, n_rows):
    t = xp.shape[0]
    assert MOE_TILE & (MOE_TILE - 1) == 0 and t % DISPATCH_TILE == 0 and DISPATCH_TILE % DMA_UNROLL == 0
    dest_blocks = dest.reshape(t // DISPATCH_TILE, 1, TOP_K * DISPATCH_TILE)
    grid_spec = pltpu.PrefetchScalarGridSpec(
        num_scalar_prefetch=3,
        grid=(t // DISPATCH_TILE,),
        in_specs=[
            pl.BlockSpec((1, 1, TOP_K * DISPATCH_TILE), lambda i, c, p, n: (i, 0, 0), memory_space=pltpu.SMEM),
            pl.BlockSpec((DISPATCH_TILE // DMA_UNROLL, DMA_UNROLL, PACK_W), lambda i, c, p, n: (i, 0, 0)),
        ],
        out_specs=pl.BlockSpec(memory_space=pl.ANY),
        scratch_shapes=[
            pltpu.VMEM((MOE_TILE, PACK_W), jnp.uint32),
            pltpu.SemaphoreType.DMA(()),
            pltpu.SemaphoreType.DMA(()),
        ],
    )
    return pl.pallas_call(
        _dispatch_kernel,
        grid_spec=grid_spec,
        out_shape=jax.ShapeDtypeStruct((n_rows, PACK_W), jnp.uint32),
        compiler_params=_cparams(1),
        name="dispatch",
    )(counts, pstart, n_used, dest_blocks, xp.reshape(t // DMA_UNROLL, DMA_UNROLL, PACK_W))


def _expert_kernel(blk_e_ref, n_used_ref, first_ref, slot_ref, next_e_ref, next2_e_ref, x_ref,
                   w1_hbm, w3_hbm, w2_hbm, y_ref, w1f_ref, w3f_ref, w2f_ref, sem,
                   w1b_ref, w3b_ref, w2b_ref, *, layer):
    b = pl.program_id(0)

    def weight_copies(e, s):
        return [pltpu.make_async_copy(w_hbm.at[layer, e], wf_ref.at[s], sem.at[s])
                for w_hbm, wf_ref in ((w1_hbm, w1f_ref), (w3_hbm, w3f_ref), (w2_hbm, w2f_ref))]

    def start_weights(e, s):
        for cp in weight_copies(e, s):
            cp.start()

    @pl.when(b < n_used_ref[0])
    def _():
        @pl.when(first_ref[b] == 1)
        def _():
            s = slot_ref[b]
            e = blk_e_ref[b]
            e_next = next_e_ref[e]
            e_next2 = next2_e_ref[e]

            @pl.when(b == 0)
            def _():
                start_weights(e, s)

                @pl.when(e_next >= 0)
                def _():
                    start_weights(e_next, 1 - s)

            for cp in weight_copies(e, s):
                cp.wait()
            w1b_ref[...] = w1f_ref[s].astype(BF16)
            w3b_ref[...] = w3f_ref[s].astype(BF16)
            w2b_ref[...] = w2f_ref[s].astype(BF16)

            @pl.when(e_next2 >= 0)
            def _():
                start_weights(e_next2, s)

        xb = _unpack_rows(x_ref[...]).astype(BF16)
        h1 = _dot(xb, w1b_ref[...])
        h3 = _dot(xb, w3b_ref[...])
        h = (h1 * jax.nn.sigmoid(h1)) * h3
        y_ref[...] = _pack_rows(_dot(h.astype(BF16), w2b_ref[...]))

    @pl.when(b >= n_used_ref[0])
    def _():
        y_ref[...] = jnp.zeros_like(y_ref)


def _experts(xbuf, blk_e, n_used, first, slot, next_e, next2_e, layer, w1, w3, w2):
    n_blocks = xbuf.shape[0] // MOE_TILE

    def x_index(b, be, nu, *_):
        return (jnp.minimum(b, jnp.maximum(nu[0] - 1, 0)), 0)

    grid_spec = pltpu.PrefetchScalarGridSpec(
        num_scalar_prefetch=6,
        grid=(n_blocks,),
        in_specs=[
            pl.BlockSpec((MOE_TILE, PACK_W), x_index),
            pl.BlockSpec(memory_space=pl.ANY),
            pl.BlockSpec(memory_space=pl.ANY),
            pl.BlockSpec(memory_space=pl.ANY),
        ],
        out_specs=pl.BlockSpec((MOE_TILE, PACK_W), lambda b, *_: (b, 0)),
        scratch_shapes=[
            pltpu.VMEM((2, D_MODEL, EXPERT_FF), F32),
            pltpu.VMEM((2, D_MODEL, EXPERT_FF), F32),
            pltpu.VMEM((2, EXPERT_FF, D_MODEL), F32),
            pltpu.SemaphoreType.DMA((2,)),
            pltpu.VMEM((D_MODEL, EXPERT_FF), BF16),
            pltpu.VMEM((D_MODEL, EXPERT_FF), BF16),
            pltpu.VMEM((EXPERT_FF, D_MODEL), BF16),
        ],
    )
    return pl.pallas_call(
        functools.partial(_expert_kernel, layer=layer),
        grid_spec=grid_spec,
        out_shape=jax.ShapeDtypeStruct(xbuf.shape, jnp.uint32),
        compiler_params=_cparams(1),
        name="experts",
    )(blk_e, n_used, first, slot, next_e, next2_e, xbuf, w1, w3, w2)


def _combine_kernel(dest_ref, dest_next_ref, info_ref, x_ref, y_hbm, g_ref, b_ref, o_ref, rows_ref, sem):
    i = pl.program_id(0)
    n_steps = pl.num_programs(0)
    tm = x_ref.shape[0]
    slot = lax.rem(i, 2)

    def row_copies(idx_ref, s, g, u):
        r = g * DMA_UNROLL + u
        return [pltpu.make_async_copy(y_hbm.at[pl.ds(idx_ref[0, 0, TOP_K * r + k], 1), :],
                                      rows_ref.at[s, k, g, pl.ds(u, 1), :], sem.at[s])
                for k in range(TOP_K)]

    def start_tile(idx_ref, s):
        def body(g, carry):
            for u in range(DMA_UNROLL):
                for cp in row_copies(idx_ref, s, g, u):
                    cp.start()
            return carry
        lax.fori_loop(0, tm // DMA_UNROLL, body, 0)

    @pl.when(i == 0)
    def _():
        start_tile(dest_ref, 0)

    @pl.when(i + 1 < n_steps)
    def _():
        start_tile(dest_next_ref, 1 - slot)

    def wait(g, carry):
        for u in range(DMA_UNROLL):
            for cp in row_copies(dest_ref, slot, g, u):
                cp.wait()
        return carry
    lax.fori_loop(0, tm // DMA_UNROLL, wait, 0)

    info = info_ref[...]
    f = None
    for k in range(TOP_K):
        rows = rows_ref[slot, k].reshape(tm, PACK_W)
        term = info[:, 4 + k:5 + k] * _unpack_rows(rows)
        f = term if f is None else f + term
    y = DN_ALPHA * x_ref[...] + f
    o_ref[...] = _layer_norm(y, g_ref[...], b_ref[...])


def _combine(dest, info, x, yb, ln_g, ln_b):
    t = x.shape[0]
    n_steps = t // ROW_TILE
    assert ROW_TILE % DMA_UNROLL == 0
    dest_blocks = dest.reshape(n_steps, 1, TOP_K * ROW_TILE)
    idx_block = (1, 1, TOP_K * ROW_TILE)
    return pl.pallas_call(
        _combine_kernel,
        grid=(n_steps,),
        in_specs=[
            pl.BlockSpec(idx_block, lambda i: (i, 0, 0), memory_space=pltpu.SMEM),
            pl.BlockSpec(idx_block, lambda i: (jnp.minimum(i + 1, n_steps - 1), 0, 0), memory_space=pltpu.SMEM),
            pl.BlockSpec((ROW_TILE, LANES), lambda i: (i, 0)),
            pl.BlockSpec((ROW_TILE, D_MODEL), lambda i: (i, 0)),
            pl.BlockSpec(memory_space=pl.ANY),
            _resident((1, D_MODEL)),
            _resident((1, D_MODEL)),
        ],
        out_specs=pl.BlockSpec((ROW_TILE, D_MODEL), lambda i: (i, 0)),
        out_shape=jax.ShapeDtypeStruct((t, D_MODEL), F32),
        scratch_shapes=[
            pltpu.VMEM((2, TOP_K, ROW_TILE // DMA_UNROLL, DMA_UNROLL, PACK_W), jnp.uint32),
            pltpu.SemaphoreType.DMA((2,)),
        ],
        compiler_params=_cparams(1),
        name="combine",
    )(dest_blocks, dest_blocks, info, x, yb, ln_g, ln_b)


def _plan_kernel(cnt_ref, pstart_ref, n_used_ref, blk_e_ref, first_ref, slot_ref, next_ref, next2_ref):
    n_blocks = blk_e_ref.shape[0]
    shift = MOE_TILE.bit_length() - 1

    def per_expert(e, carry):
        block, ordinal = carry
        n_blk = (cnt_ref[e] + (MOE_TILE - 1)) >> shift
        pstart_ref[e] = block << shift

        def per_block(j, c):
            blk_e_ref[block + j] = e
            first_ref[block + j] = jnp.where(j == 0, 1, 0)
            slot_ref[block + j] = ordinal & 1
            return c
        lax.fori_loop(0, n_blk, per_block, 0)
        return block + n_blk, ordinal + jnp.where(n_blk > 0, 1, 0)
    n_used, _ = lax.fori_loop(0, N_EXPERTS, per_expert, (jnp.int32(0), jnp.int32(0)))
    n_used_ref[0] = n_used

    def unused(b, c):
        blk_e_ref[b] = N_EXPERTS - 1
        first_ref[b] = 0
        slot_ref[b] = 0
        return c
    lax.fori_loop(n_used, n_blocks, unused, 0)

    def backwards(k, carry):
        nxt, nxt2 = carry
        e = N_EXPERTS - 1 - k
        next_ref[e] = nxt
        next2_ref[e] = nxt2
        used = cnt_ref[e] > 0
        return jnp.where(used, e, nxt), jnp.where(used, nxt, nxt2)
    lax.fori_loop(0, N_EXPERTS, backwards, (jnp.int32(-1), jnp.int32(-1)))


def _plan(counts, n_blocks):
    smem = pl.BlockSpec(memory_space=pltpu.SMEM)
    i32 = jnp.int32
    return pl.pallas_call(
        _plan_kernel,
        in_specs=[smem],
        out_specs=[smem] * 7,
        out_shape=[
            jax.ShapeDtypeStruct((N_EXPERTS,), i32),
            jax.ShapeDtypeStruct((1,), i32),
            jax.ShapeDtypeStruct((n_blocks,), i32),
            jax.ShapeDtypeStruct((n_blocks,), i32),
            jax.ShapeDtypeStruct((n_blocks,), i32),
            jax.ShapeDtypeStruct((N_EXPERTS,), i32),
            jax.ShapeDtypeStruct((N_EXPERTS,), i32),
        ],
        name="moe_plan",
    )(counts)


def _moe(x, xp, wg1, bg1, wg2, bg2, layer, w1, w3, w2, ln_g, ln_b):
    t = x.shape[0]
    w_r = jnp.concatenate([wg1, jnp.transpose(wg2, (1, 0, 2)).reshape(D_MODEL, N_EXPERTS)], axis=1)
    w_r = jnp.pad(w_r, ((0, 0), (0, LANES - w_r.shape[1])))
    b_r = jnp.pad(jnp.concatenate([bg1, bg2.reshape(-1)]), (0, LANES - N_GROUPS - N_EXPERTS))[None, :]
    info, cnt = _router(x, w_r, b_r)

    expert = info[:, 0:2].astype(jnp.int32)
    rank = info[:, 2:4].astype(jnp.int32)
    counts = cnt[0, ROUTER_LANE0:ROUTER_LANE0 + N_EXPERTS].astype(jnp.int32)
    n_blocks = (t * TOP_K) // MOE_TILE + N_EXPERTS
    pstart, n_used, blk_e, first, slot, next_e, next2_e = _plan(counts, n_blocks)
    experts = jnp.arange(N_EXPERTS, dtype=jnp.int32)
    dest = jnp.sum(jnp.where(expert[:, :, None] == experts, pstart, 0), axis=-1) + rank

    xbuf = _dispatch(counts, pstart, n_used, dest, xp, n_blocks * MOE_TILE)
    yb = _experts(xbuf, blk_e, n_used, first, slot, next_e, next2_e, layer, w1, w3, w2)
    return _combine(dest, info, x, yb, ln_g, ln_b)


def kernel(x, mem, a_w_in, a_sgu_ln_g, a_sgu_ln_b, a_ws, a_bs, a_w_out, b_w_in, b_lambda, b_subln_g, b_w_out, shared_w_kv, rel_bias, mem_w_kv, ln_g, ln_b, moe_wg1, moe_bg1, moe_wg2, moe_bg2, moe_w1, moe_w3, moe_w2):
    b_, s_, d_ = x.shape
    assert b_ == 1 and d_ == D_MODEL
    h = x.reshape(s_, d_)
    kt_mem, v_mem = _memkv(mem.reshape(MEM_TOKENS, d_), mem_w_kv)
    bias_table = _near_bias_tables(rel_bias)
    shared_k = shared_v = None
    for l in range(DEPTH):
        if l < N_A_LAYERS:
            i = l
            mix, qm = _pre_a(h, a_w_in[i].astype(BF16), a_sgu_ln_g[i][None, :], a_sgu_ln_b[i][None, :],
                             a_ws[i], jnp.transpose(a_bs[i]))
            w_out = a_w_out[i]
        else:
            i = l - N_A_LAYERS
            qd, qm = _pre_b(h, jnp.transpose(b_w_in[i][:, :DIFF_QK_W]).astype(BF16),
                            b_w_in[i][:, DIFF_QK_W:].astype(BF16))
            lambda_init = 0.8 - 0.6 * math.exp(-0.3 * l)
            mix = _attention(qd, shared_k, shared_v, bias_table, b_lambda[i], b_subln_g[i][None, :],
                             lambda_init)
            w_out = b_w_out[i]
        h, hp = _post(mix, qm, kt_mem, v_mem, l, h, w_out.astype(BF16), ln_g[l, 0][None, :], ln_b[l, 0][None, :])
        h = _moe(h, hp, moe_wg1[l], moe_bg1[l], moe_wg2[l], moe_bg2[l], l, moe_w1, moe_w3, moe_w2,
                 ln_g[l, 1][None, :], ln_b[l, 1][None, :])
        if l == N_A_LAYERS - 1:
            shared_k, shared_v = _kvproj(h, shared_w_kv[:, :DIFF_QK_W].astype(BF16),
                                         jnp.transpose(shared_w_kv[:, DIFF_QK_W:]).astype(BF16))
    return h.reshape(b_, s_, d_)
```

```python
import functools
import math

import jax
import jax.numpy as jnp
from jax import lax
from jax.experimental import pallas as pl
from jax.experimental.pallas import tpu as pltpu

F32 = jnp.float32
BF16 = jnp.bfloat16

D_MODEL = 2048
DEPTH = 4
CHUNK = 64
N_A_LAYERS = DEPTH // 2
HEAD_DIM = 128
MEM_TOKENS = 256
MEM_HEADS = 4
MEM_W = MEM_HEADS * HEAD_DIM
MIX_W = D_MODEL - MEM_W
SGU_CHUNK = 128
SGU_GROUPS = 4
SGU_GROUP_W = MIX_W // SGU_GROUPS
DIFF_HEADS = MIX_W // (2 * HEAD_DIM)
DIFF_QK_W = 2 * DIFF_HEADS * HEAD_DIM
DIFF_V_DIM = 2 * HEAD_DIM
REL_BUCKETS = 32
REL_MAX_DIST = 128
N_GROUPS = 4
EXPERTS_PER_GROUP = 8
N_EXPERTS = N_GROUPS * EXPERTS_PER_GROUP
TOP_K = 2
EXPERT_FF = 512
DN_ALPHA = (2 * DEPTH) ** 0.25
LN_EPS = 1e-5
QK_SCALE = HEAD_DIM ** -0.5
SQRT_HALF = math.sqrt(0.5)
LOG2E = math.log2(math.e)
NEG_BIG = -1e30

LANES = 128
VMEM_LIMIT_BYTES = 56 * 1024 * 1024

ROW_TILE = 512
ROUTER_TILE = 512
ATT_TILE = 256
ATT_FAR = 2
ATT_NEAR = ATT_FAR + 1
MOE_TILE = 128
ROUTER_LANE0 = N_GROUPS


def _cparams(n_axes=1):
    return pltpu.CompilerParams(
        dimension_semantics=("arbitrary",) * n_axes,
        vmem_limit_bytes=VMEM_LIMIT_BYTES,
    )


def _dot(a, b):
    return jnp.dot(a, b, preferred_element_type=F32)


def _dot_nt(a, b):
    return lax.dot_general(a, b, (((1,), (1,)), ((), ())), preferred_element_type=F32)


def _gelu(x):
    return 0.5 * x * (1.0 + lax.erf(x * SQRT_HALF))


def _layer_norm(y, g, b):
    mu = jnp.mean(y, axis=-1, keepdims=True)
    d = y - mu
    var = jnp.mean(d * d, axis=-1, keepdims=True)
    return d * lax.rsqrt(var + LN_EPS) * g + b


HI16 = 0xFFFF0000
PACK_W = D_MODEL // 2


def _pack_rows(y):
    lo = y[:, :PACK_W].astype(BF16).astype(F32)
    hi = y[:, PACK_W:].astype(BF16).astype(F32)
    return (pltpu.bitcast(lo, jnp.uint32) >> 16) | (pltpu.bitcast(hi, jnp.uint32) & jnp.uint32(HI16))


def _unpack_rows(w):
    lo = pltpu.bitcast(w << 16, F32)
    hi = pltpu.bitcast(w & jnp.uint32(HI16), F32)
    return jnp.concatenate([lo, hi], axis=-1)


def _resident(shape):
    nd = len(shape)
    return pl.BlockSpec(shape, lambda *_: (0,) * nd, pipeline_mode=pl.Buffered(1))


def _memkv_kernel(mem_ref, w_ref, kt_ref, v_ref):
    kv = _dot(mem_ref[...].astype(BF16), w_ref[0].astype(BF16))
    kt_ref[0] = kv[:, :MEM_W].T.astype(BF16)
    v_ref[0] = kv[:, MEM_W:].astype(BF16)


def _memkv(mem, mem_w_kv):
    n_layers = mem_w_kv.shape[0]
    m = mem.shape[0]
    return pl.pallas_call(
        _memkv_kernel,
        grid=(n_layers,),
        in_specs=[
            pl.BlockSpec((m, D_MODEL), lambda l: (0, 0)),
            pl.BlockSpec((1, D_MODEL, 2 * MEM_W), lambda l: (l, 0, 0)),
        ],
        out_specs=[
            pl.BlockSpec((1, MEM_W, m), lambda l: (l, 0, 0)),
            pl.BlockSpec((1, m, MEM_W), lambda l: (l, 0, 0)),
        ],
        out_shape=[
            jax.ShapeDtypeStruct((n_layers, MEM_W, m), BF16),
            jax.ShapeDtypeStruct((n_layers, m, MEM_W), BF16),
        ],
        compiler_params=_cparams(1),
        name="memkv",
    )(mem, mem_w_kv)


def _pre_a_kernel(x_ref, w_ref, lng_ref, lnb_ref, ws_ref, bst_ref, mix_ref, qm_ref, vn_ref):
    tm = x_ref.shape[0]
    x = x_ref[...].astype(BF16)
    v = _gelu(_dot(x, w_ref[:, MIX_W:2 * MIX_W]))
    vn_ref[...] = _layer_norm(v, lng_ref[...], lnb_ref[...]).astype(BF16)
    row = lax.broadcasted_iota(jnp.int32, (SGU_CHUNK, SGU_CHUNK), 0)
    col = lax.broadcasted_iota(jnp.int32, (SGU_CHUNK, SGU_CHUNK), 1)
    shift = CHUNK.bit_length() - 1
    keep = (col >> shift) <= (row >> shift)
    for g in range(SGU_GROUPS):
        cols = slice(g * SGU_GROUP_W, (g + 1) * SGU_GROUP_W)
        w_sp = jnp.where(keep, ws_ref[g], 0.0).astype(BF16)
        u = _gelu(_dot(x, w_ref[:, cols]))
        bias = bst_ref[:, g:g + 1]
        for c in range(tm // SGU_CHUNK):
            rows = slice(c * SGU_CHUNK, (c + 1) * SGU_CHUNK)
            gate = _dot(w_sp, vn_ref[rows, cols]) + bias
            mix_ref[rows, cols] = (u[rows] * gate).astype(BF16)
    qm_ref[...] = (_dot(x, w_ref[:, 2 * MIX_W:]) * QK_SCALE).astype(BF16)


def _pre_a(x, w_in, ln_g, ln_b, ws, bs_t):
    t = x.shape[0]
    n_in = w_in.shape[1]
    return pl.pallas_call(
        _pre_a_kernel,
        grid=(t // ROW_TILE,),
        in_specs=[
            pl.BlockSpec((ROW_TILE, D_MODEL), lambda i: (i, 0)),
            _resident((D_MODEL, n_in)),
            _resident((1, MIX_W)),
            _resident((1, MIX_W)),
            _resident((SGU_GROUPS, SGU_CHUNK, SGU_CHUNK)),
            _resident((SGU_CHUNK, SGU_GROUPS)),
        ],
        out_specs=[
            pl.BlockSpec((ROW_TILE, MIX_W), lambda i: (i, 0)),
            pl.BlockSpec((ROW_TILE, MEM_W), lambda i: (i, 0)),
        ],
        out_shape=[
            jax.ShapeDtypeStruct((t, MIX_W), BF16),
            jax.ShapeDtypeStruct((t, MEM_W), BF16),
        ],
        scratch_shapes=[pltpu.VMEM((ROW_TILE, MIX_W), BF16)],
        compiler_params=_cparams(1),
        name="pre_a",
    )(x, w_in, ln_g, ln_b, ws, bs_t)


def _pre_b_kernel(x_ref, wqt_ref, wm_ref, qt_ref, qm_ref):
    x = x_ref[...].astype(BF16)
    for h in range(DIFF_HEADS):
        rows = slice(h * 2 * HEAD_DIM, (h + 1) * 2 * HEAD_DIM)
        qt_ref[h, 0] = (_dot_nt(wqt_ref[rows, :], x) * (QK_SCALE * LOG2E)).astype(BF16)
    qm_ref[...] = (_dot(x, wm_ref[...]) * QK_SCALE).astype(BF16)


def _pre_b(x, w_q_t, w_mem):
    t = x.shape[0]
    return pl.pallas_call(
        _pre_b_kernel,
        grid=(t // ATT_TILE,),
        in_specs=[
            pl.BlockSpec((ATT_TILE, D_MODEL), lambda i: (i, 0)),
            _resident((DIFF_QK_W, D_MODEL)),
            _resident((D_MODEL, MEM_W)),
        ],
        out_specs=[
            pl.BlockSpec((DIFF_HEADS, 1, 2 * HEAD_DIM, ATT_TILE), lambda i: (0, i, 0, 0)),
            pl.BlockSpec((ATT_TILE, MEM_W), lambda i: (i, 0)),
        ],
        out_shape=[
            jax.ShapeDtypeStruct((DIFF_HEADS, t // ATT_TILE, 2 * HEAD_DIM, ATT_TILE), BF16),
            jax.ShapeDtypeStruct((t, MEM_W), BF16),
        ],
        compiler_params=_cparams(1),
        name="pre_b",
    )(x, w_q_t, w_mem)


def _kvproj_kernel(x_ref, wk_ref, wvt_ref, k_ref, vt_ref):
    x = x_ref[...].astype(BF16)
    k_ref[...] = _dot(x, wk_ref[...]).astype(BF16)
    for h in range(DIFF_HEADS):
        rows = slice(h * DIFF_V_DIM, (h + 1) * DIFF_V_DIM)
        vt_ref[h, 0] = _dot_nt(wvt_ref[rows, :], x).astype(BF16)


def _kvproj(x, w_k, w_v_t):
    t = x.shape[0]
    n_v = DIFF_HEADS * DIFF_V_DIM
    return pl.pallas_call(
        _kvproj_kernel,
        grid=(t // ATT_TILE,),
        in_specs=[
            pl.BlockSpec((ATT_TILE, D_MODEL), lambda i: (i, 0)),
            _resident((D_MODEL, DIFF_QK_W)),
            _resident((n_v, D_MODEL)),
        ],
        out_specs=[
            pl.BlockSpec((ATT_TILE, DIFF_QK_W), lambda i: (i, 0)),
            pl.BlockSpec((DIFF_HEADS, 1, DIFF_V_DIM, ATT_TILE), lambda i: (0, i, 0, 0)),
        ],
        out_shape=[
            jax.ShapeDtypeStruct((t, DIFF_QK_W), BF16),
            jax.ShapeDtypeStruct((DIFF_HEADS, t // ATT_TILE, DIFF_V_DIM, ATT_TILE), BF16),
        ],
        compiler_params=_cparams(1),
        name="kvproj",
    )(x, w_k, w_v_t)


def _attn_kernel(qt_ref, k_ref, vt_ref, bias_ref, lam_ref, g_ref, o_ref, acc1_ref, acc2_ref,
                 s_ref, p_ref, *, lambda_init):
    assert ATT_FAR in (1, 2) and ATT_NEAR == ATT_FAR + 1
    i = pl.program_id(1)
    tq = qt_ref.shape[3]
    acc1_ref[...] = jnp.zeros_like(acc1_ref)
    acc2_ref[...] = jnp.zeros_like(acc2_ref)

    def scores(j0, n_blk, slot):
        start = pl.multiple_of(j0 * tq, tq)
        kblk = k_ref[pl.ds(start, n_blk * tq), :]
        for s in range(2):
            rows = slice(s * HEAD_DIM, (s + 1) * HEAD_DIM)
            s_ref[slot, s, :n_blk * tq, :] = _dot(kblk[:, rows], qt_ref[0, 0, rows, :])

    def probs(n_blk, bias, stats, slot):
        new_stats, alphas = [], []
        for s in range(2):
            m, l = stats[2 * s], stats[2 * s + 1]
            st = s_ref[slot, s, :n_blk * tq, :]
            if bias is not None:
                st = st + bias
            m_new = jnp.maximum(m, jnp.max(st, axis=0, keepdims=True))
            alpha = jnp.exp2(m - m_new)
            p = jnp.exp2(st - m_new)
            p_ref[slot, s, :n_blk * tq, :] = p.astype(BF16)
            new_stats += [m_new, alpha * l + jnp.sum(p, axis=0, keepdims=True)]
            alphas.append(alpha)
        return tuple(new_stats), tuple(alphas)

    def accumulate(j0, n_blk, alphas, slot):
        vts = [vt_ref[0, j0 + n] for n in range(n_blk)]
        for s, acc_ref in enumerate((acc1_ref, acc2_ref)):
            pv = _dot(vts[0], p_ref[slot, s, :tq, :])
            for n in range(1, n_blk):
                pv = pv + _dot(vts[n], p_ref[slot, s, n * tq:(n + 1) * tq, :])
            acc_ref[...] = alphas[s] * acc_ref[...] + pv

    neg = jnp.full((1, tq), NEG_BIG, F32)
    zero = jnp.zeros((1, tq), F32)
    stats = (neg, zero, neg, zero)
    fb = ATT_FAR
    n_far = jnp.maximum(i - 1, 0) // fb
    near_j0 = jnp.maximum(i + 1 - ATT_NEAR, 0)
    near_slot = 2
    near_bias = bias_ref[0, 0]

    def near_scores():
        scores(near_j0, ATT_NEAR, near_slot)

    def near_probs(stats):
        return probs(ATT_NEAR, near_bias, stats, near_slot)

    def only_near(stats):
        near_scores()
        stats, alphas = near_probs(stats)
        accumulate(near_j0, ATT_NEAR, alphas, near_slot)
        return stats

    def far_then_near(stats):
        scores(0, fb, 0)

        def first_two(stats):
            out = probs(fb, None, stats, 0)
            scores(fb, fb, 1)
            return out

        def first_and_near(stats):
            out = probs(fb, None, stats, 0)
            near_scores()
            return out
        carry = lax.cond(n_far >= 2, first_two, first_and_near, stats)

        def steady(t, carry, parity):
            stats, alphas = carry
            accumulate(fb * (t - 2), fb, alphas, parity)
            out = probs(fb, None, stats, 1 - parity)
            scores(fb * t, fb, parity)
            return out

        def body(u, carry):
            t = 2 + 2 * u
            return steady(t + 1, steady(t, carry, 0), 1)
        n_steady = jnp.maximum(n_far - 2, 0)
        carry = lax.fori_loop(0, n_steady // 2, body, carry)
        carry = lax.cond(lax.rem(n_steady, 2) == 1,
                         lambda c: steady(n_far - 1, c, 0), lambda c: c, carry)

        def finish(carry, parity):
            def last_far_and_near(carry):
                stats, alphas = carry
                accumulate(fb * (n_far - 2), fb, alphas, parity)
                out = probs(fb, None, stats, 1 - parity)
                near_scores()
                return out
            stats, alphas = lax.cond(n_far >= 2, last_far_and_near, lambda c: c, carry)
            accumulate(fb * (n_far - 1), fb, alphas, 1 - parity)
            stats, alphas = near_probs(stats)
            accumulate(near_j0, ATT_NEAR, alphas, near_slot)
            return stats

        return lax.cond(lax.rem(n_far, 2) == 0, lambda c: finish(c, 0), lambda c: finish(c, 1), carry)

    m1, l1, m2, l2 = lax.cond(n_far > 0, far_then_near, only_near, stats)

    lp = lam_ref[...]
    lam = (jnp.exp(jnp.sum(lp[0:1] * lp[1:2], axis=-1, keepdims=True))
           - jnp.exp(jnp.sum(lp[2:3] * lp[3:4], axis=-1, keepdims=True)) + lambda_init)
    ot = acc1_ref[...] / l1 - lam * (acc2_ref[...] / l2)
    ms = jnp.mean(ot * ot, axis=0, keepdims=True)
    ot = ot * lax.rsqrt(ms + LN_EPS)
    o_ref[...] = (ot.T * (g_ref[...] * (1.0 - lambda_init))).astype(o_ref.dtype)


def _attention(qt, k, vt, bias, lam_params, subln_g, lambda_init):
    t = k.shape[0]
    n_blk = t // ATT_TILE
    kern = functools.partial(_attn_kernel, lambda_init=lambda_init)
    return pl.pallas_call(
        kern,
        grid=(DIFF_HEADS, n_blk),
        in_specs=[
            pl.BlockSpec((1, 1, 2 * HEAD_DIM, ATT_TILE), lambda h, i: (h, i, 0, 0)),
            pl.BlockSpec((t, 2 * HEAD_DIM), lambda h, i: (0, h)),
            pl.BlockSpec((1, n_blk, DIFF_V_DIM, ATT_TILE), lambda h, i: (h, 0, 0, 0)),
            pl.BlockSpec((1, 1, ATT_NEAR * ATT_TILE, ATT_TILE),
                         lambda h, i: (h, _near_table_variant(i), 0, 0)),
            pl.BlockSpec((4, HEAD_DIM), lambda h, i: (0, 0)),
            pl.BlockSpec((1, DIFF_V_DIM), lambda h, i: (0, 0)),
        ],
        out_specs=pl.BlockSpec((ATT_TILE, DIFF_V_DIM), lambda h, i: (i, h)),
        out_shape=jax.ShapeDtypeStruct((t, DIFF_HEADS * DIFF_V_DIM), BF16),
        scratch_shapes=[pltpu.VMEM((DIFF_V_DIM, ATT_TILE), F32),
                        pltpu.VMEM((DIFF_V_DIM, ATT_TILE), F32),
                        pltpu.VMEM((3, 2, ATT_NEAR * ATT_TILE, ATT_TILE), F32),
                        pltpu.VMEM((3, 2, ATT_NEAR * ATT_TILE, ATT_TILE), BF16)],
        compiler_params=_cparams(2),
        name="diff_attn",
    )(qt, k, vt, bias, lam_params, subln_g)


def _relative_bucket(rel):
    n = REL_BUCKETS // 2
    max_exact = n // 2
    ret = jnp.where(rel > 0, n, 0)
    a = jnp.abs(rel)
    af = jnp.maximum(a, 1).astype(jnp.float32)
    large = max_exact + (jnp.log(af / max_exact) / math.log(REL_MAX_DIST / max_exact)
                         * (n - max_exact)).astype(jnp.int32)
    large = jnp.minimum(large, n - 1)
    return ret + jnp.where(a < max_exact, a, large)


def _near_table_variant(i):
    return jnp.where(i < ATT_NEAR - 1, i, ATT_NEAR - 1 + lax.rem(i - (ATT_NEAR - 1), ATT_FAR))


def _near_bias_tables(rel_bias):
    tq = ATT_TILE
    assert tq >= REL_MAX_DIST and tq % CHUNK == 0 and ATT_NEAR == ATT_FAR + 1
    qpos = jnp.arange(tq)[:, None]
    kpos = jnp.arange(ATT_NEAR * tq)[None, :] - (ATT_NEAR - 1) * tq
    onehot = jax.nn.one_hot(_relative_bucket(kpos - qpos), REL_BUCKETS, dtype=F32)
    bias = jnp.einsum('qkb,bh->qkh', onehot, rel_bias.astype(F32),
                      precision=lax.Precision.HIGHEST)
    far = rel_bias[_relative_bucket(jnp.array(-ATT_NEAR * tq))].astype(F32)
    visible = (kpos // CHUNK) <= (qpos // CHUNK)
    table = jnp.where(visible[:, :, None], (bias - far) * LOG2E, NEG_BIG)
    base = jnp.transpose(table, (2, 1, 0))
    masked = jnp.full_like(base[:, :tq], NEG_BIG)
    variants = []
    for i in range(ATT_NEAR - 1):
        lead = ATT_NEAR - 1 - i
        variants.append(jnp.concatenate([base[:, lead * tq:]] + [masked] * lead, axis=1))
    for covered in range(ATT_FAR):
        variants.append(jnp.concatenate([masked] * covered + [base[:, covered * tq:]], axis=1))
    return jnp.stack(variants, axis=1)


def _post_kernel(mix_ref, qm_ref, kt_ref, vm_ref, x_ref, w_ref, g_ref, b_ref, o_ref, op_ref):
    heads = []
    for h in range(MEM_HEADS):
        cols = slice(h * HEAD_DIM, (h + 1) * HEAD_DIM)
        s = _dot(qm_ref[:, cols], kt_ref[0, cols, :])
        p = jnp.exp(s - jnp.max(s, axis=-1, keepdims=True))
        l = jnp.sum(p, axis=-1, keepdims=True)
        heads.append((_dot(p.astype(BF16), vm_ref[0, :, cols]) / l).astype(BF16))
    mem_out = jnp.concatenate(heads, axis=-1)
    t = _dot(mix_ref[...], w_ref[:MIX_W, :]) + _dot(mem_out, w_ref[MIX_W:, :])
    y = _layer_norm(DN_ALPHA * x_ref[...] + t, g_ref[...], b_ref[...])
    o_ref[...] = y
    op_ref[...] = _pack_rows(y)


def _post(mix, qm, kt, vm, layer, x, w_out, ln_g, ln_b):
    t = x.shape[0]
    m = kt.shape[2]
    return pl.pallas_call(
        _post_kernel,
        grid=(t // ROW_TILE,),
        in_specs=[
            pl.BlockSpec((ROW_TILE, MIX_W), lambda i: (i, 0)),
            pl.BlockSpec((ROW_TILE, MEM_W), lambda i: (i, 0)),
            pl.BlockSpec((1, MEM_W, m), lambda i: (layer, 0, 0), pipeline_mode=pl.Buffered(1)),
            pl.BlockSpec((1, m, MEM_W), lambda i: (layer, 0, 0), pipeline_mode=pl.Buffered(1)),
            pl.BlockSpec((ROW_TILE, D_MODEL), lambda i: (i, 0)),
            _resident((D_MODEL, D_MODEL)),
            _resident((1, D_MODEL)),
            _resident((1, D_MODEL)),
        ],
        out_specs=[
            pl.BlockSpec((ROW_TILE, D_MODEL), lambda i: (i, 0)),
            pl.BlockSpec((ROW_TILE, PACK_W), lambda i: (i, 0)),
        ],
        out_shape=[
            jax.ShapeDtypeStruct((t, D_MODEL), F32),
            jax.ShapeDtypeStruct((t, PACK_W), jnp.uint32),
        ],
        compiler_params=_cparams(1),
        name="post",
    )(mix, qm, kt, vm, x, w_out, ln_g, ln_b)


def _router_kernel(x_ref, w_ref, b_ref, info_ref, cnt_ref, carry_ref, wsplit_ref):
    step = pl.program_id(0)
    tm = x_ref.shape[0]

    @pl.when(step == 0)
    def _():
        carry_ref[...] = jnp.zeros_like(carry_ref)
        w = w_ref[...]
        wh = w.astype(BF16)
        wsplit_ref[:, :LANES] = wh
        wsplit_ref[:, LANES:] = (w - wh.astype(F32)).astype(BF16)

    x = x_ref[...]
    xh = x.astype(BF16)
    xl = (x - xh.astype(F32)).astype(BF16)
    both = _dot(xh, wsplit_ref[...])
    logits = both[:, :LANES] + (both[:, LANES:] + _dot(xl, wsplit_ref[:, :LANES])) + b_ref[...]
    lane = lax.broadcasted_iota(jnp.int32, (tm, LANES), 1).astype(F32)
    no_lane = float(LANES)

    def top(mask):
        val = jnp.max(jnp.where(mask, logits, -jnp.inf), axis=-1, keepdims=True)
        idx = jnp.min(jnp.where(mask & (logits == val), lane, no_lane), axis=-1, keepdims=True)
        return val, idx

    is_group = lane < float(N_GROUPS)
    g_val, g_idx = top(is_group)
    p_sel = 1.0 / jnp.sum(jnp.where(is_group, jnp.exp(logits - g_val), 0.0), axis=-1, keepdims=True)
    first = float(ROUTER_LANE0) + float(EXPERTS_PER_GROUP) * g_idx
    in_group = (lane >= first) & (lane < first + float(EXPERTS_PER_GROUP))
    v1, i1 = top(in_group)
    v2, i2 = top(in_group & (lane != i1))
    e2 = jnp.exp(v2 - v1)
    gate1 = p_sel / (1.0 + e2)
    gate2 = p_sel * e2 / (1.0 + e2)

    hit1 = lane == i1
    hit2 = lane == i2
    onehot = jnp.where(hit1 | hit2, 1.0, 0.0)
    r = lax.broadcasted_iota(jnp.int32, (tm, tm), 0)
    c = lax.broadcasted_iota(jnp.int32, (tm, tm), 1)
    strict_lower = jnp.where(c < r, 1.0, 0.0).astype(BF16)
    before = _dot(strict_lower, onehot.astype(BF16)) + carry_ref[0:1, :]
    rank1 = jnp.sum(jnp.where(hit1, before, 0.0), axis=-1, keepdims=True)
    rank2 = jnp.sum(jnp.where(hit2, before, 0.0), axis=-1, keepdims=True)
    total = carry_ref[0:1, :] + jnp.sum(onehot, axis=0, keepdims=True)
    carry_ref[...] = jnp.broadcast_to(total, carry_ref.shape)
    cnt_ref[...] = jnp.broadcast_to(total, cnt_ref.shape)

    vals = (i1 - float(ROUTER_LANE0), i2 - float(ROUTER_LANE0), rank1, rank2, gate1, gate2)
    info = jnp.zeros((tm, LANES), F32)
    for k, val in enumerate(vals):
        info = jnp.where(lane == float(k), val, info)
    info_ref[...] = info


def _router(x, w_r, b_r):
    t = x.shape[0]
    return pl.pallas_call(
        _router_kernel,
        grid=(t // ROUTER_TILE,),
        in_specs=[
            pl.BlockSpec((ROUTER_TILE, D_MODEL), lambda i: (i, 0)),
            _resident((D_MODEL, LANES)),
            _resident((1, LANES)),
        ],
        out_specs=[
            pl.BlockSpec((ROUTER_TILE, LANES), lambda i: (i, 0)),
            pl.BlockSpec((8, LANES), lambda i: (0, 0)),
        ],
        out_shape=[
            jax.ShapeDtypeStruct((t, LANES), F32),
            jax.ShapeDtypeStruct((8, LANES), F32),
        ],
        scratch_shapes=[pltpu.VMEM((8, LANES), F32), pltpu.VMEM((D_MODEL, 2 * LANES), BF16)],
        compiler_params=_cparams(1),
        name="router",
    )(x, w_r, b_r)


DISPATCH_TILE = 512
DMA_UNROLL = 8


def _dispatch_kernel(cnt_ref, pstart_ref, n_used_ref, dest_ref, xp_ref, xbuf_hbm, zero_ref, sem, zsem):
    step = pl.program_id(0)
    n_tok = dest_ref.shape[2] // TOP_K
    n_blocks = xbuf_hbm.shape[0] // MOE_TILE

    def token_copies(g, u):
        r = g * DMA_UNROLL + u
        return [pltpu.make_async_copy(xp_ref.at[g, pl.ds(u, 1), :],
                                      xbuf_hbm.at[pl.ds(dest_ref[0, 0, TOP_K * r + k], 1), :], sem)
                for k in range(TOP_K)]

    def pad_rows(e):
        first = pstart_ref[e] + cnt_ref[e]
        n_pad = (-cnt_ref[e]) & (MOE_TILE - 1)
        return first, n_pad

    def zero_copy(row):
        return pltpu.make_async_copy(zero_ref.at[pl.ds(0, 1), :], xbuf_hbm.at[pl.ds(row, 1), :], zsem)

    def zero_block_copy(b):
        return pltpu.make_async_copy(zero_ref, xbuf_hbm.at[pl.ds(b * MOE_TILE, MOE_TILE), :], zsem)

    @pl.when(step == 0)
    def _():
        zero_ref[...] = jnp.zeros_like(zero_ref)

        def start_expert(e, carry):
            first, n_pad = pad_rows(e)
            lax.fori_loop(0, n_pad, lambda r, c: (zero_copy(first + r).start(), c)[1], 0)
            return carry
        lax.fori_loop(0, N_EXPERTS, start_expert, 0)
        lax.fori_loop(n_used_ref[0], n_blocks, lambda b, c: (zero_block_copy(b).start(), c)[1], 0)

    def start(g, carry):
        for u in range(DMA_UNROLL):
            for cp in token_copies(g, u):
                cp.start()
        return carry
    lax.fori_loop(0, n_tok // DMA_UNROLL, start, 0)

    def wait(g, carry):
        for u in range(DMA_UNROLL):
            for cp in token_copies(g, u):
                cp.wait()
        return carry
    lax.fori_loop(0, n_tok // DMA_UNROLL, wait, 0)

    @pl.when(step == 0)
    def _():
        def wait_expert(e, carry):
            first, n_pad = pad_rows(e)
            lax.fori_loop(0, n_pad, lambda r, c: (zero_copy(first + r).wait(), c)[1], 0)
            return carry
        lax.fori_loop(0, N_EXPERTS, wait_expert, 0)
        lax.fori_loop(n_used_ref[0], n_blocks, lambda b, c: (zero_block_copy(b).wait(), c)[1], 0)


def _dispatch(counts, pstart, n_used, dest, xp, n_rows):
    t = xp.shape[0]
    assert MOE_TILE & (MOE_TILE - 1) == 0 and t % DISPATCH_TILE == 0 and DISPATCH_TILE % DMA_UNROLL == 0
    dest_blocks = dest.reshape(t // DISPATCH_TILE, 1, TOP_K * DISPATCH_TILE)
    grid_spec = pltpu.PrefetchScalarGridSpec(
        num_scalar_prefetch=3,
        grid=(t // DISPATCH_TILE,),
        in_specs=[
            pl.BlockSpec((1, 1, TOP_K * DISPATCH_TILE), lambda i, c, p, n: (i, 0, 0), memory_space=pltpu.SMEM),
            pl.BlockSpec((DISPATCH_TILE // DMA_UNROLL, DMA_UNROLL, PACK_W), lambda i, c, p, n: (i, 0, 0)),
        ],
        out_specs=pl.BlockSpec(memory_space=pl.ANY),
        scratch_shapes=[
            pltpu.VMEM((MOE_TILE, PACK_W), jnp.uint32),
            pltpu.SemaphoreType.DMA(()),
            pltpu.SemaphoreType.DMA(()),
        ],
    )
    return pl.pallas_call(
        _dispatch_kernel,
        grid_spec=grid_spec,
        out_shape=jax.ShapeDtypeStruct((n_rows, PACK_W), jnp.uint32),
        compiler_params=_cparams(1),
        name="dispatch",
    )(counts, pstart, n_used, dest_blocks, xp.reshape(t // DMA_UNROLL, DMA_UNROLL, PACK_W))


def _expert_kernel(blk_e_ref, n_used_ref, first_ref, slot_ref, next_e_ref, next2_e_ref, x_ref,
                   w1_hbm, w3_hbm, w2_hbm, y_ref, w1f_ref, w3f_ref, w2f_ref, sem,
                   w1b_ref, w3b_ref, w2b_ref, *, layer):
    b = pl.program_id(0)

    def weight_copies(e, s):
        return [pltpu.make_async_copy(w_hbm.at[layer, e], wf_ref.at[s], sem.at[s])
                for w_hbm, wf_ref in ((w1_hbm, w1f_ref), (w3_hbm, w3f_ref), (w2_hbm, w2f_ref))]

    def start_weights(e, s):
        for cp in weight_copies(e, s):
            cp.start()

    @pl.when(b < n_used_ref[0])
    def _():
        @pl.when(first_ref[b] == 1)
        def _():
            s = slot_ref[b]
            e = blk_e_ref[b]
            e_next = next_e_ref[e]
            e_next2 = next2_e_ref[e]

            @pl.when(b == 0)
            def _():
                start_weights(e, s)

                @pl.when(e_next >= 0)
                def _():
                    start_weights(e_next, 1 - s)

            for cp in weight_copies(e, s):
                cp.wait()
            w1b_ref[...] = w1f_ref[s].astype(BF16)
            w3b_ref[...] = w3f_ref[s].astype(BF16)
            w2b_ref[...] = w2f_ref[s].astype(BF16)

            @pl.when(e_next2 >= 0)
            def _():
                start_weights(e_next2, s)

        xb = _unpack_rows(x_ref[...]).astype(BF16)
        h1 = _dot(xb, w1b_ref[...])
        h3 = _dot(xb, w3b_ref[...])
        h = (h1 * jax.nn.sigmoid(h1)) * h3
        y_ref[...] = _pack_rows(_dot(h.astype(BF16), w2b_ref[...]))

    @pl.when(b >= n_used_ref[0])
    def _():
        y_ref[...] = jnp.zeros_like(y_ref)


def _experts(xbuf, blk_e, n_used, first, slot, next_e, next2_e, layer, w1, w3, w2):
    n_blocks = xbuf.shape[0] // MOE_TILE

    def x_index(b, be, nu, *_):
        return (jnp.minimum(b, jnp.maximum(nu[0] - 1, 0)), 0)

    grid_spec = pltpu.PrefetchScalarGridSpec(
        num_scalar_prefetch=6,
        grid=(n_blocks,),
        in_specs=[
            pl.BlockSpec((MOE_TILE, PACK_W), x_index),
            pl.BlockSpec(memory_space=pl.ANY),
            pl.BlockSpec(memory_space=pl.ANY),
            pl.BlockSpec(memory_space=pl.ANY),
        ],
        out_specs=pl.BlockSpec((MOE_TILE, PACK_W), lambda b, *_: (b, 0)),
        scratch_shapes=[
            pltpu.VMEM((2, D_MODEL, EXPERT_FF), F32),
            pltpu.VMEM((2, D_MODEL, EXPERT_FF), F32),
            pltpu.VMEM((2, EXPERT_FF, D_MODEL), F32),
            pltpu.SemaphoreType.DMA((2,)),
            pltpu.VMEM((D_MODEL, EXPERT_FF), BF16),
            pltpu.VMEM((D_MODEL, EXPERT_FF), BF16),
            pltpu.VMEM((EXPERT_FF, D_MODEL), BF16),
        ],
    )
    return pl.pallas_call(
        functools.partial(_expert_kernel, layer=layer),
        grid_spec=grid_spec,
        out_shape=jax.ShapeDtypeStruct(xbuf.shape, jnp.uint32),
        compiler_params=_cparams(1),
        name="experts",
    )(blk_e, n_used, first, slot, next_e, next2_e, xbuf, w1, w3, w2)


def _combine_kernel(dest_ref, dest_next_ref, info_ref, x_ref, y_hbm, g_ref, b_ref, o_ref, rows_ref, sem):
    i = pl.program_id(0)
    n_steps = pl.num_programs(0)
    tm = x_ref.shape[0]
    slot = lax.rem(i, 2)

    def row_copies(idx_ref, s, g, u):
        r = g * DMA_UNROLL + u
        return [pltpu.make_async_copy(y_hbm.at[pl.ds(idx_ref[0, 0, TOP_K * r + k], 1), :],
                                      rows_ref.at[s, k, g, pl.ds(u, 1), :], sem.at[s])
                for k in range(TOP_K)]

    def start_tile(idx_ref, s):
        def body(g, carry):
            for u in range(DMA_UNROLL):
                for cp in row_copies(idx_ref, s, g, u):
                    cp.start()
            return carry
        lax.fori_loop(0, tm // DMA_UNROLL, body, 0)

    @pl.when(i == 0)
    def _():
        start_tile(dest_ref, 0)

    @pl.when(i + 1 < n_steps)
    def _():
        start_tile(dest_next_ref, 1 - slot)

    def wait(g, carry):
        for u in range(DMA_UNROLL):
            for cp in row_copies(dest_ref, slot, g, u):
                cp.wait()
        return carry
    lax.fori_loop(0, tm // DMA_UNROLL, wait, 0)

    info = info_ref[...]
    f = None
    for k in range(TOP_K):
        rows = rows_ref[slot, k].reshape(tm, PACK_W)
        term = info[:, 4 + k:5 + k] * _unpack_rows(rows)
        f = term if f is None else f + term
    y = DN_ALPHA * x_ref[...] + f
    o_ref[...] = _layer_norm(y, g_ref[...], b_ref[...])


def _combine(dest, info, x, yb, ln_g, ln_b):
    t = x.shape[0]
    n_steps = t // ROW_TILE
    assert ROW_TILE % DMA_UNROLL == 0
    dest_blocks = dest.reshape(n_steps, 1, TOP_K * ROW_TILE)
    idx_block = (1, 1, TOP_K * ROW_TILE)
    return pl.pallas_call(
        _combine_kernel,
        grid=(n_steps,),
        in_specs=[
            pl.BlockSpec(idx_block, lambda i: (i, 0, 0), memory_space=pltpu.SMEM),
            pl.BlockSpec(idx_block, lambda i: (jnp.minimum(i + 1, n_steps - 1), 0, 0), memory_space=pltpu.SMEM),
            pl.BlockSpec((ROW_TILE, LANES), lambda i: (i, 0)),
            pl.BlockSpec((ROW_TILE, D_MODEL), lambda i: (i, 0)),
            pl.BlockSpec(memory_space=pl.ANY),
            _resident((1, D_MODEL)),
            _resident((1, D_MODEL)),
        ],
        out_specs=pl.BlockSpec((ROW_TILE, D_MODEL), lambda i: (i, 0)),
        out_shape=jax.ShapeDtypeStruct((t, D_MODEL), F32),
        scratch_shapes=[
            pltpu.VMEM((2, TOP_K, ROW_TILE // DMA_UNROLL, DMA_UNROLL, PACK_W), jnp.uint32),
            pltpu.SemaphoreType.DMA((2,)),
        ],
        compiler_params=_cparams(1),
        name="combine",
    )(dest_blocks, dest_blocks, info, x, yb, ln_g, ln_b)


def _plan_kernel(cnt_ref, pstart_ref, n_used_ref, blk_e_ref, first_ref, slot_ref, next_ref, next2_ref):
    n_blocks = blk_e_ref.shape[0]
    shift = MOE_TILE.bit_length() - 1

    def per_expert(e, carry):
        block, ordinal = carry
        n_blk = (cnt_ref[e] + (MOE_TILE - 1)) >> shift
        pstart_ref[e] = block << shift

        def per_block(j, c):
            blk_e_ref[block + j] = e
            first_ref[block + j] = jnp.where(j == 0, 1, 0)
            slot_ref[block + j] = ordinal & 1
            return c
        lax.fori_loop(0, n_blk, per_block, 0)
        return block + n_blk, ordinal + jnp.where(n_blk > 0, 1, 0)
    n_used, _ = lax.fori_loop(0, N_EXPERTS, per_expert, (jnp.int32(0), jnp.int32(0)))
    n_used_ref[0] = n_used

    def unused(b, c):
        blk_e_ref[b] = N_EXPERTS - 1
        first_ref[b] = 0
        slot_ref[b] = 0
        return c
    lax.fori_loop(n_used, n_blocks, unused, 0)

    def backwards(k, carry):
        nxt, nxt2 = carry
        e = N_EXPERTS - 1 - k
        next_ref[e] = nxt
        next2_ref[e] = nxt2
        used = cnt_ref[e] > 0
        return jnp.where(used, e, nxt), jnp.where(used, nxt, nxt2)
    lax.fori_loop(0, N_EXPERTS, backwards, (jnp.int32(-1), jnp.int32(-1)))


def _plan(counts, n_blocks):
    smem = pl.BlockSpec(memory_space=pltpu.SMEM)
    i32 = jnp.int32
    return pl.pallas_call(
        _plan_kernel,
        in_specs=[smem],
        out_specs=[smem] * 7,
        out_shape=[
            jax.ShapeDtypeStruct((N_EXPERTS,), i32),
            jax.ShapeDtypeStruct((1,), i32),
            jax.ShapeDtypeStruct((n_blocks,), i32),
            jax.ShapeDtypeStruct((n_blocks,), i32),
            jax.ShapeDtypeStruct((n_blocks,), i32),
            jax.ShapeDtypeStruct((N_EXPERTS,), i32),
            jax.ShapeDtypeStruct((N_EXPERTS,), i32),
        ],
        name="moe_plan",
    )(counts)


def _moe(x, xp, wg1, bg1, wg2, bg2, layer, w1, w3, w2, ln_g, ln_b):
    t = x.shape[0]
    w_r = jnp.concatenate([wg1, jnp.transpose(wg2, (1, 0, 2)).reshape(D_MODEL, N_EXPERTS)], axis=1)
    w_r = jnp.pad(w_r, ((0, 0), (0, LANES - w_r.shape[1])))
    b_r = jnp.pad(jnp.concatenate([bg1, bg2.reshape(-1)]), (0, LANES - N_GROUPS - N_EXPERTS))[None, :]
    info, cnt = _router(x, w_r, b_r)

    expert = info[:, 0:2].astype(jnp.int32)
    rank = info[:, 2:4].astype(jnp.int32)
    counts = cnt[0, ROUTER_LANE0:ROUTER_LANE0 + N_EXPERTS].astype(jnp.int32)
    n_blocks = (t * TOP_K) // MOE_TILE + N_EXPERTS
    pstart, n_used, blk_e, first, slot, next_e, next2_e = _plan(counts, n_blocks)
    experts = jnp.arange(N_EXPERTS, dtype=jnp.int32)
    dest = jnp.sum(jnp.where(expert[:, :, None] == experts, pstart, 0), axis=-1) + rank

    xbuf = _dispatch(counts, pstart, n_used, dest, xp, n_blocks * MOE_TILE)
    yb = _experts(xbuf, blk_e, n_used, first, slot, next_e, next2_e, layer, w1, w3, w2)
    return _combine(dest, info, x, yb, ln_g, ln_b)


def kernel(x, mem, a_w_in, a_sgu_ln_g, a_sgu_ln_b, a_ws, a_bs, a_w_out, b_w_in, b_lambda, b_subln_g, b_w_out, shared_w_kv, rel_bias, mem_w_kv, ln_g, ln_b, moe_wg1, moe_bg1, moe_wg2, moe_bg2, moe_w1, moe_w3, moe_w2):
    b_, s_, d_ = x.shape
    assert b_ == 1 and d_ == D_MODEL
    h = x.reshape(s_, d_)
    kt_mem, v_mem = _memkv(mem.reshape(MEM_TOKENS, d_), mem_w_kv)
    bias_table = _near_bias_tables(rel_bias)
    shared_k = shared_v = None
    for l in range(DEPTH):
        if l < N_A_LAYERS:
            i = l
            mix, qm = _pre_a(h, a_w_in[i].astype(BF16), a_sgu_ln_g[i][None, :], a_sgu_ln_b[i][None, :],
                             a_ws[i], jnp.transpose(a_bs[i]))
            w_out = a_w_out[i]
        else:
            i = l - N_A_LAYERS
            qd, qm = _pre_b(h, jnp.transpose(b_w_in[i][:, :DIFF_QK_W]).astype(BF16),
                            b_w_in[i][:, DIFF_QK_W:].astype(BF16))
            lambda_init = 0.8 - 0.6 * math.exp(-0.3 * l)
            mix = _attention(qd, shared_k, shared_v, bias_table, b_lambda[i], b_subln_g[i][None, :],
                             lambda_init)
            w_out = b_w_out[i]
        h, hp = _post(mix, qm, kt_mem, v_mem, l, h, w_out.astype(BF16), ln_g[l, 0][None, :], ln_b[l, 0][None, :])
        h = _moe(h, hp, moe_wg1[l], moe_bg1[l], moe_wg2[l], moe_bg2[l], l, moe_w1, moe_w3, moe_w2,
                 ln_g[l, 1][None, :], ln_b[l, 1][None, :])
        if l == N_A_LAYERS - 1:
            shared_k, shared_v = _kvproj(h, shared_w_kv[:, :DIFF_QK_W].astype(BF16),
                                         jnp.transpose(shared_w_kv[:, DIFF_QK_W:]).astype(BF16))
    return h.reshape(b_, s_, d_)
```

```python
import functools
import math

import jax
import jax.numpy as jnp
from jax import lax
from jax.experimental import pallas as pl
from jax.experimental.pallas import tpu as pltpu

F32 = jnp.float32
BF16 = jnp.bfloat16

D_MODEL = 2048
DEPTH = 4
CHUNK = 64
N_A_LAYERS = DEPTH // 2
HEAD_DIM = 128
MEM_TOKENS = 256
MEM_HEADS = 4
MEM_W = MEM_HEADS * HEAD_DIM
MIX_W = D_MODEL - MEM_W
SGU_CHUNK = 128
SGU_GROUPS = 4
SGU_GROUP_W = MIX_W // SGU_GROUPS
DIFF_HEADS = MIX_W // (2 * HEAD_DIM)
DIFF_QK_W = 2 * DIFF_HEADS * HEAD_DIM
DIFF_V_DIM = 2 * HEAD_DIM
REL_BUCKETS = 32
REL_MAX_DIST = 128
N_GROUPS = 4
EXPERTS_PER_GROUP = 8
N_EXPERTS = N_GROUPS * EXPERTS_PER_GROUP
TOP_K = 2
EXPERT_FF = 512
DN_ALPHA = (2 * DEPTH) ** 0.25
LN_EPS = 1e-5
QK_SCALE = HEAD_DIM ** -0.5
SQRT_HALF = math.sqrt(0.5)
LOG2E = math.log2(math.e)
NEG_BIG = -1e30

LANES = 128
VMEM_LIMIT_BYTES = 56 * 1024 * 1024

ROW_TILE = 256
ROUTER_TILE = 512
ATT_TILE = 256
ATT_FAR = 2
ATT_NEAR = ATT_FAR + 1
MOE_TILE = 128
ROUTER_LANE0 = N_GROUPS


def _cparams(n_axes=1):
    return pltpu.CompilerParams(
        dimension_semantics=("arbitrary",) * n_axes,
        vmem_limit_bytes=VMEM_LIMIT_BYTES,
    )


def _dot(a, b):
    return jnp.dot(a, b, preferred_element_type=F32)


def _dot_nt(a, b):
    return lax.dot_general(a, b, (((1,), (1,)), ((), ())), preferred_element_type=F32)


def _gelu(x):
    return 0.5 * x * (1.0 + lax.erf(x * SQRT_HALF))


def _layer_norm(y, g, b):
    mu = jnp.mean(y, axis=-1, keepdims=True)
    d = y - mu
    var = jnp.mean(d * d, axis=-1, keepdims=True)
    return d * lax.rsqrt(var + LN_EPS) * g + b


HI16 = 0xFFFF0000
PACK_W = D_MODEL // 2


def _pack_rows(y):
    lo = y[:, :PACK_W].astype(BF16).astype(F32)
    hi = y[:, PACK_W:].astype(BF16).astype(F32)
    return (pltpu.bitcast(lo, jnp.uint32) >> 16) | (pltpu.bitcast(hi, jnp.uint32) & jnp.uint32(HI16))


def _unpack_rows(w):
    lo = pltpu.bitcast(w << 16, F32)
    hi = pltpu.bitcast(w & jnp.uint32(HI16), F32)
    return jnp.concatenate([lo, hi], axis=-1)


def _resident(shape):
    nd = len(shape)
    return pl.BlockSpec(shape, lambda *_: (0,) * nd, pipeline_mode=pl.Buffered(1))


def _memkv_kernel(mem_ref, w_ref, kt_ref, v_ref):
    kv = _dot(mem_ref[...].astype(BF16), w_ref[0].astype(BF16))
    kt_ref[0] = kv[:, :MEM_W].T.astype(BF16)
    v_ref[0] = kv[:, MEM_W:].astype(BF16)


def _memkv(mem, mem_w_kv):
    n_layers = mem_w_kv.shape[0]
    m = mem.shape[0]
    return pl.pallas_call(
        _memkv_kernel,
        grid=(n_layers,),
        in_specs=[
            pl.BlockSpec((m, D_MODEL), lambda l: (0, 0)),
            pl.BlockSpec((1, D_MODEL, 2 * MEM_W), lambda l: (l, 0, 0)),
        ],
        out_specs=[
            pl.BlockSpec((1, MEM_W, m), lambda l: (l, 0, 0)),
            pl.BlockSpec((1, m, MEM_W), lambda l: (l, 0, 0)),
        ],
        out_shape=[
            jax.ShapeDtypeStruct((n_layers, MEM_W, m), BF16),
            jax.ShapeDtypeStruct((n_layers, m, MEM_W), BF16),
        ],
        compiler_params=_cparams(1),
        name="memkv",
    )(mem, mem_w_kv)


def _pre_a_kernel(x_ref, w_ref, lng_ref, lnb_ref, ws_ref, bst_ref, mix_ref, qm_ref, vn_ref):
    tm = x_ref.shape[0]
    x = x_ref[...].astype(BF16)
    v = _gelu(_dot(x, w_ref[:, MIX_W:2 * MIX_W]))
    vn_ref[...] = _layer_norm(v, lng_ref[...], lnb_ref[...]).astype(BF16)
    row = lax.broadcasted_iota(jnp.int32, (SGU_CHUNK, SGU_CHUNK), 0)
    col = lax.broadcasted_iota(jnp.int32, (SGU_CHUNK, SGU_CHUNK), 1)
    shift = CHUNK.bit_length() - 1
    keep = (col >> shift) <= (row >> shift)
    for g in range(SGU_GROUPS):
        cols = slice(g * SGU_GROUP_W, (g + 1) * SGU_GROUP_W)
        w_sp = jnp.where(keep, ws_ref[g], 0.0).astype(BF16)
        u = _gelu(_dot(x, w_ref[:, cols]))
        bias = bst_ref[:, g:g + 1]
        for c in range(tm // SGU_CHUNK):
            rows = slice(c * SGU_CHUNK, (c + 1) * SGU_CHUNK)
            gate = _dot(w_sp, vn_ref[rows, cols]) + bias
            mix_ref[rows, cols] = (u[rows] * gate).astype(BF16)
    qm_ref[...] = (_dot(x, w_ref[:, 2 * MIX_W:]) * QK_SCALE).astype(BF16)


def _pre_a(x, w_in, ln_g, ln_b, ws, bs_t):
    t = x.shape[0]
    n_in = w_in.shape[1]
    return pl.pallas_call(
        _pre_a_kernel,
        grid=(t // ROW_TILE,),
        in_specs=[
            pl.BlockSpec((ROW_TILE, D_MODEL), lambda i: (i, 0)),
            _resident((D_MODEL, n_in)),
            _resident((1, MIX_W)),
            _resident((1, MIX_W)),
            _resident((SGU_GROUPS, SGU_CHUNK, SGU_CHUNK)),
            _resident((SGU_CHUNK, SGU_GROUPS)),
        ],
        out_specs=[
            pl.BlockSpec((ROW_TILE, MIX_W), lambda i: (i, 0)),
            pl.BlockSpec((ROW_TILE, MEM_W), lambda i: (i, 0)),
        ],
        out_shape=[
            jax.ShapeDtypeStruct((t, MIX_W), BF16),
            jax.ShapeDtypeStruct((t, MEM_W), BF16),
        ],
        scratch_shapes=[pltpu.VMEM((ROW_TILE, MIX_W), BF16)],
        compiler_params=_cparams(1),
        name="pre_a",
    )(x, w_in, ln_g, ln_b, ws, bs_t)


def _pre_b_kernel(x_ref, wqt_ref, wm_ref, qt_ref, qm_ref):
    x = x_ref[...].astype(BF16)
    for h in range(DIFF_HEADS):
        rows = slice(h * 2 * HEAD_DIM, (h + 1) * 2 * HEAD_DIM)
        qt_ref[h, 0] = (_dot_nt(wqt_ref[rows, :], x) * (QK_SCALE * LOG2E)).astype(BF16)
    qm_ref[...] = (_dot(x, wm_ref[...]) * QK_SCALE).astype(BF16)


def _pre_b(x, w_q_t, w_mem):
    t = x.shape[0]
    return pl.pallas_call(
        _pre_b_kernel,
        grid=(t // ATT_TILE,),
        in_specs=[
            pl.BlockSpec((ATT_TILE, D_MODEL), lambda i: (i, 0)),
            _resident((DIFF_QK_W, D_MODEL)),
            _resident((D_MODEL, MEM_W)),
        ],
        out_specs=[
            pl.BlockSpec((DIFF_HEADS, 1, 2 * HEAD_DIM, ATT_TILE), lambda i: (0, i, 0, 0)),
            pl.BlockSpec((ATT_TILE, MEM_W), lambda i: (i, 0)),
        ],
        out_shape=[
            jax.ShapeDtypeStruct((DIFF_HEADS, t // ATT_TILE, 2 * HEAD_DIM, ATT_TILE), BF16),
            jax.ShapeDtypeStruct((t, MEM_W), BF16),
        ],
        compiler_params=_cparams(1),
        name="pre_b",
    )(x, w_q_t, w_mem)


def _kvproj_kernel(x_ref, wk_ref, wvt_ref, k_ref, vt_ref):
    x = x_ref[...].astype(BF16)
    k_ref[...] = _dot(x, wk_ref[...]).astype(BF16)
    for h in range(DIFF_HEADS):
        rows = slice(h * DIFF_V_DIM, (h + 1) * DIFF_V_DIM)
        vt_ref[h, 0] = _dot_nt(wvt_ref[rows, :], x).astype(BF16)


def _kvproj(x, w_k, w_v_t):
    t = x.shape[0]
    n_v = DIFF_HEADS * DIFF_V_DIM
    return pl.pallas_call(
        _kvproj_kernel,
        grid=(t // ATT_TILE,),
        in_specs=[
            pl.BlockSpec((ATT_TILE, D_MODEL), lambda i: (i, 0)),
            _resident((D_MODEL, DIFF_QK_W)),
            _resident((n_v, D_MODEL)),
        ],
        out_specs=[
            pl.BlockSpec((ATT_TILE, DIFF_QK_W), lambda i: (i, 0)),
            pl.BlockSpec((DIFF_HEADS, 1, DIFF_V_DIM, ATT_TILE), lambda i: (0, i, 0, 0)),
        ],
        out_shape=[
            jax.ShapeDtypeStruct((t, DIFF_QK_W), BF16),
            jax.ShapeDtypeStruct((DIFF_HEADS, t // ATT_TILE, DIFF_V_DIM, ATT_TILE), BF16),
        ],
        compiler_params=_cparams(1),
        name="kvproj",
    )(x, w_k, w_v_t)


def _attn_kernel(qt_ref, k_ref, vt_ref, bias_ref, lam_ref, g_ref, o_ref, acc1_ref, acc2_ref,
                 s_ref, p_ref, *, lambda_init):
    assert ATT_FAR in (1, 2) and ATT_NEAR == ATT_FAR + 1
    i = pl.program_id(1)
    tq = qt_ref.shape[3]
    acc1_ref[...] = jnp.zeros_like(acc1_ref)
    acc2_ref[...] = jnp.zeros_like(acc2_ref)

    def scores(j0, n_blk, slot):
        start = pl.multiple_of(j0 * tq, tq)
        kblk = k_ref[pl.ds(start, n_blk * tq), :]
        for s in range(2):
            rows = slice(s * HEAD_DIM, (s + 1) * HEAD_DIM)
            s_ref[slot, s, :n_blk * tq, :] = _dot(kblk[:, rows], qt_ref[0, 0, rows, :])

    def probs(n_blk, bias, stats, slot):
        new_stats, alphas = [], []
        for s in range(2):
            m, l = stats[2 * s], stats[2 * s + 1]
            st = s_ref[slot, s, :n_blk * tq, :]
            if bias is not None:
                st = st + bias
            m_new = jnp.maximum(m, jnp.max(st, axis=0, keepdims=True))
            alpha = jnp.exp2(m - m_new)
            p = jnp.exp2(st - m_new)
            p_ref[slot, s, :n_blk * tq, :] = p.astype(BF16)
            new_stats += [m_new, alpha * l + jnp.sum(p, axis=0, keepdims=True)]
            alphas.append(alpha)
        return tuple(new_stats), tuple(alphas)

    def accumulate(j0, n_blk, alphas, slot):
        vts = [vt_ref[0, j0 + n] for n in range(n_blk)]
        for s, acc_ref in enumerate((acc1_ref, acc2_ref)):
            pv = _dot(vts[0], p_ref[slot, s, :tq, :])
            for n in range(1, n_blk):
                pv = pv + _dot(vts[n], p_ref[slot, s, n * tq:(n + 1) * tq, :])
            acc_ref[...] = alphas[s] * acc_ref[...] + pv

    neg = jnp.full((1, tq), NEG_BIG, F32)
    zero = jnp.zeros((1, tq), F32)
    stats = (neg, zero, neg, zero)
    fb = ATT_FAR
    n_far = jnp.maximum(i - 1, 0) // fb
    near_j0 = jnp.maximum(i + 1 - ATT_NEAR, 0)
    near_slot = 2
    near_bias = bias_ref[0, 0]

    def near_scores():
        scores(near_j0, ATT_NEAR, near_slot)

    def near_probs(stats):
        return probs(ATT_NEAR, near_bias, stats, near_slot)

    def only_near(stats):
        near_scores()
        stats, alphas = near_probs(stats)
        accumulate(near_j0, ATT_NEAR, alphas, near_slot)
        return stats

    def far_then_near(stats):
        scores(0, fb, 0)

        def first_two(stats):
            out = probs(fb, None, stats, 0)
            scores(fb, fb, 1)
            return out

        def first_and_near(stats):
            out = probs(fb, None, stats, 0)
            near_scores()
            return out
        carry = lax.cond(n_far >= 2, first_two, first_and_near, stats)

        def steady(t, carry, parity):
            stats, alphas = carry
            accumulate(fb * (t - 2), fb, alphas, parity)
            out = probs(fb, None, stats, 1 - parity)
            scores(fb * t, fb, parity)
            return out

        def body(u, carry):
            t = 2 + 2 * u
            return steady(t + 1, steady(t, carry, 0), 1)
        n_steady = jnp.maximum(n_far - 2, 0)
        carry = lax.fori_loop(0, n_steady // 2, body, carry)
        carry = lax.cond(lax.rem(n_steady, 2) == 1,
                         lambda c: steady(n_far - 1, c, 0), lambda c: c, carry)

        def finish(carry, parity):
            def last_far_and_near(carry):
                stats, alphas = carry
                accumulate(fb * (n_far - 2), fb, alphas, parity)
                out = probs(fb, None, stats, 1 - parity)
                near_scores()
                return out
            stats, alphas = lax.cond(n_far >= 2, last_far_and_near, lambda c: c, carry)
            accumulate(fb * (n_far - 1), fb, alphas, 1 - parity)
            stats, alphas = near_probs(stats)
            accumulate(near_j0, ATT_NEAR, alphas, near_slot)
            return stats

        return lax.cond(lax.rem(n_far, 2) == 0, lambda c: finish(c, 0), lambda c: finish(c, 1), carry)

    m1, l1, m2, l2 = lax.cond(n_far > 0, far_then_near, only_near, stats)

    lp = lam_ref[...]
    lam = (jnp.exp(jnp.sum(lp[0:1] * lp[1:2], axis=-1, keepdims=True))
           - jnp.exp(jnp.sum(lp[2:3] * lp[3:4], axis=-1, keepdims=True)) + lambda_init)
    ot = acc1_ref[...] / l1 - lam * (acc2_ref[...] / l2)
    ms = jnp.mean(ot * ot, axis=0, keepdims=True)
    ot = ot * lax.rsqrt(ms + LN_EPS)
    o_ref[...] = (ot.T * (g_ref[...] * (1.0 - lambda_init))).astype(o_ref.dtype)


def _attention(qt, k, vt, bias, lam_params, subln_g, lambda_init):
    t = k.shape[0]
    n_blk = t // ATT_TILE
    kern = functools.partial(_attn_kernel, lambda_init=lambda_init)
    return pl.pallas_call(
        kern,
        grid=(DIFF_HEADS, n_blk),
        in_specs=[
            pl.BlockSpec((1, 1, 2 * HEAD_DIM, ATT_TILE), lambda h, i: (h, i, 0, 0)),
            pl.BlockSpec((t, 2 * HEAD_DIM), lambda h, i: (0, h)),
            pl.BlockSpec((1, n_blk, DIFF_V_DIM, ATT_TILE), lambda h, i: (h, 0, 0, 0)),
            pl.BlockSpec((1, 1, ATT_NEAR * ATT_TILE, ATT_TILE),
                         lambda h, i: (h, _near_table_variant(i), 0, 0)),
            pl.BlockSpec((4, HEAD_DIM), lambda h, i: (0, 0)),
            pl.BlockSpec((1, DIFF_V_DIM), lambda h, i: (0, 0)),
        ],
        out_specs=pl.BlockSpec((ATT_TILE, DIFF_V_DIM), lambda h, i: (i, h)),
        out_shape=jax.ShapeDtypeStruct((t, DIFF_HEADS * DIFF_V_DIM), BF16),
        scratch_shapes=[pltpu.VMEM((DIFF_V_DIM, ATT_TILE), F32),
                        pltpu.VMEM((DIFF_V_DIM, ATT_TILE), F32),
                        pltpu.VMEM((3, 2, ATT_NEAR * ATT_TILE, ATT_TILE), F32),
                        pltpu.VMEM((3, 2, ATT_NEAR * ATT_TILE, ATT_TILE), BF16)],
        compiler_params=_cparams(2),
        name="diff_attn",
    )(qt, k, vt, bias, lam_params, subln_g)


def _relative_bucket(rel):
    n = REL_BUCKETS // 2
    max_exact = n // 2
    ret = jnp.where(rel > 0, n, 0)
    a = jnp.abs(rel)
    af = jnp.maximum(a, 1).astype(jnp.float32)
    large = max_exact + (jnp.log(af / max_exact) / math.log(REL_MAX_DIST / max_exact)
                         * (n - max_exact)).astype(jnp.int32)
    large = jnp.minimum(large, n - 1)
    return ret + jnp.where(a < max_exact, a, large)


def _near_table_variant(i):
    return jnp.where(i < ATT_NEAR - 1, i, ATT_NEAR - 1 + lax.rem(i - (ATT_NEAR - 1), ATT_FAR))


def _near_bias_tables(rel_bias):
    tq = ATT_TILE
    assert tq >= REL_MAX_DIST and tq % CHUNK == 0 and ATT_NEAR == ATT_FAR + 1
    qpos = jnp.arange(tq)[:, None]
    kpos = jnp.arange(ATT_NEAR * tq)[None, :] - (ATT_NEAR - 1) * tq
    onehot = jax.nn.one_hot(_relative_bucket(kpos - qpos), REL_BUCKETS, dtype=F32)
    bias = jnp.einsum('qkb,bh->qkh', onehot, rel_bias.astype(F32),
                      precision=lax.Precision.HIGHEST)
    far = rel_bias[_relative_bucket(jnp.array(-ATT_NEAR * tq))].astype(F32)
    visible = (kpos // CHUNK) <= (qpos // CHUNK)
    table = jnp.where(visible[:, :, None], (bias - far) * LOG2E, NEG_BIG)
    base = jnp.transpose(table, (2, 1, 0))
    masked = jnp.full_like(base[:, :tq], NEG_BIG)
    variants = []
    for i in range(ATT_NEAR - 1):
        lead = ATT_NEAR - 1 - i
        variants.append(jnp.concatenate([base[:, lead * tq:]] + [masked] * lead, axis=1))
    for covered in range(ATT_FAR):
        variants.append(jnp.concatenate([masked] * covered + [base[:, covered * tq:]], axis=1))
    return jnp.stack(variants, axis=1)


def _post_kernel(mix_ref, qm_ref, kt_ref, vm_ref, x_ref, w_ref, g_ref, b_ref, o_ref, op_ref):
    heads = []
    for h in range(MEM_HEADS):
        cols = slice(h * HEAD_DIM, (h + 1) * HEAD_DIM)
        s = _dot(qm_ref[:, cols], kt_ref[0, cols, :])
        p = jnp.exp(s - jnp.max(s, axis=-1, keepdims=True))
        l = jnp.sum(p, axis=-1, keepdims=True)
        heads.append((_dot(p.astype(BF16), vm_ref[0, :, cols]) / l).astype(BF16))
    mem_out = jnp.concatenate(heads, axis=-1)
    t = _dot(mix_ref[...], w_ref[:MIX_W, :]) + _dot(mem_out, w_ref[MIX_W:, :])
    y = _layer_norm(DN_ALPHA * x_ref[...] + t, g_ref[...], b_ref[...])
    o_ref[...] = y
    op_ref[...] = _pack_rows(y)


def _post(mix, qm, kt, vm, layer, x, w_out, ln_g, ln_b):
    t = x.shape[0]
    m = kt.shape[2]
    return pl.pallas_call(
        _post_kernel,
        grid=(t // ROW_TILE,),
        in_specs=[
            pl.BlockSpec((ROW_TILE, MIX_W), lambda i: (i, 0)),
            pl.BlockSpec((ROW_TILE, MEM_W), lambda i: (i, 0)),
            pl.BlockSpec((1, MEM_W, m), lambda i: (layer, 0, 0), pipeline_mode=pl.Buffered(1)),
            pl.BlockSpec((1, m, MEM_W), lambda i: (layer, 0, 0), pipeline_mode=pl.Buffered(1)),
            pl.BlockSpec((ROW_TILE, D_MODEL), lambda i: (i, 0)),
            _resident((D_MODEL, D_MODEL)),
            _resident((1, D_MODEL)),
            _resident((1, D_MODEL)),
        ],
        out_specs=[
            pl.BlockSpec((ROW_TILE, D_MODEL), lambda i: (i, 0)),
            pl.BlockSpec((ROW_TILE, PACK_W), lambda i: (i, 0)),
        ],
        out_shape=[
            jax.ShapeDtypeStruct((t, D_MODEL), F32),
            jax.ShapeDtypeStruct((t, PACK_W), jnp.uint32),
        ],
        compiler_params=_cparams(1),
        name="post",
    )(mix, qm, kt, vm, x, w_out, ln_g, ln_b)


def _router_kernel(x_ref, w_ref, b_ref, info_ref, cnt_ref, carry_ref, wsplit_ref):
    step = pl.program_id(0)
    tm = x_ref.shape[0]

    @pl.when(step == 0)
    def _():
        carry_ref[...] = jnp.zeros_like(carry_ref)
        w = w_ref[...]
        wh = w.astype(BF16)
        wsplit_ref[:, :LANES] = wh
        wsplit_ref[:, LANES:] = (w - wh.astype(F32)).astype(BF16)

    x = x_ref[...]
    xh = x.astype(BF16)
    xl = (x - xh.astype(F32)).astype(BF16)
    both = _dot(xh, wsplit_ref[...])
    logits = both[:, :LANES] + (both[:, LANES:] + _dot(xl, wsplit_ref[:, :LANES])) + b_ref[...]
    lane = lax.broadcasted_iota(jnp.int32, (tm, LANES), 1).astype(F32)
    no_lane = float(LANES)

    def top(mask):
        val = jnp.max(jnp.where(mask, logits, -jnp.inf), axis=-1, keepdims=True)
        idx = jnp.min(jnp.where(mask & (logits == val), lane, no_lane), axis=-1, keepdims=True)
        return val, idx

    is_group = lane < float(N_GROUPS)
    g_val, g_idx = top(is_group)
    p_sel = 1.0 / jnp.sum(jnp.where(is_group, jnp.exp(logits - g_val), 0.0), axis=-1, keepdims=True)
    first = float(ROUTER_LANE0) + float(EXPERTS_PER_GROUP) * g_idx
    in_group = (lane >= first) & (lane < first + float(EXPERTS_PER_GROUP))
    v1, i1 = top(in_group)
    v2, i2 = top(in_group & (lane != i1))
    e2 = jnp.exp(v2 - v1)
    gate1 = p_sel / (1.0 + e2)
    gate2 = p_sel * e2 / (1.0 + e2)

    hit1 = lane == i1
    hit2 = lane == i2
    onehot = jnp.where(hit1 | hit2, 1.0, 0.0)
    r = lax.broadcasted_iota(jnp.int32, (tm, tm), 0)
    c = lax.broadcasted_iota(jnp.int32, (tm, tm), 1)
    strict_lower = jnp.where(c < r, 1.0, 0.0).astype(BF16)
    before = _dot(strict_lower, onehot.astype(BF16)) + carry_ref[0:1, :]
    rank1 = jnp.sum(jnp.where(hit1, before, 0.0), axis=-1, keepdims=True)
    rank2 = jnp.sum(jnp.where(hit2, before, 0.0), axis=-1, keepdims=True)
    total = carry_ref[0:1, :] + jnp.sum(onehot, axis=0, keepdims=True)
    carry_ref[...] = jnp.broadcast_to(total, carry_ref.shape)
    cnt_ref[...] = jnp.broadcast_to(total, cnt_ref.shape)

    vals = (i1 - float(ROUTER_LANE0), i2 - float(ROUTER_LANE0), rank1, rank2, gate1, gate2)
    info = jnp.zeros((tm, LANES), F32)
    for k, val in enumerate(vals):
        info = jnp.where(lane == float(k), val, info)
    info_ref[...] = info


def _router(x, w_r, b_r):
    t = x.shape[0]
    return pl.pallas_call(
        _router_kernel,
        grid=(t // ROUTER_TILE,),
        in_specs=[
            pl.BlockSpec((ROUTER_TILE, D_MODEL), lambda i: (i, 0)),
            _resident((D_MODEL, LANES)),
            _resident((1, LANES)),
        ],
        out_specs=[
            pl.BlockSpec((ROUTER_TILE, LANES), lambda i: (i, 0)),
            pl.BlockSpec((8, LANES), lambda i: (0, 0)),
        ],
        out_shape=[
            jax.ShapeDtypeStruct((t, LANES), F32),
            jax.ShapeDtypeStruct((8, LANES), F32),
        ],
        scratch_shapes=[pltpu.VMEM((8, LANES), F32), pltpu.VMEM((D_MODEL, 2 * LANES), BF16)],
        compiler_params=_cparams(1),
        name="router",
    )(x, w_r, b_r)


DISPATCH_TILE = 2048
DMA_UNROLL = 8


def _dispatch_kernel(cnt_ref, pstart_ref, n_used_ref, dest_ref, xp_ref, xbuf_hbm, zero_ref, sem, zsem):
    step = pl.program_id(0)
    n_tok = dest_ref.shape[2] // TOP_K
    n_blocks = xbuf_hbm.shape[0] // MOE_TILE

    def token_copies(g, u):
        r = g * DMA_UNROLL + u
        return [pltpu.make_async_copy(xp_ref.at[g, pl.ds(u, 1), :],
                                      xbuf_hbm.at[pl.ds(dest_ref[0, 0, TOP_K * r + k], 1), :], sem)
                for k in range(TOP_K)]

    def pad_rows(e):
        first = pstart_ref[e] + cnt_ref[e]
        n_pad = (-cnt_ref[e]) & (MOE_TILE - 1)
        return first, n_pad

    def zero_copy(row):
        return pltpu.make_async_copy(zero_ref.at[pl.ds(0, 1), :], xbuf_hbm.at[pl.ds(row, 1), :], zsem)

    def zero_block_copy(b):
        return pltpu.make_async_copy(zero_ref, xbuf_hbm.at[pl.ds(b * MOE_TILE, MOE_TILE), :], zsem)

    @pl.when(step == 0)
    def _():
        zero_ref[...] = jnp.zeros_like(zero_ref)

        def start_expert(e, carry):
            first, n_pad = pad_rows(e)
            lax.fori_loop(0, n_pad, lambda r, c: (zero_copy(first + r).start(), c)[1], 0)
            return carry
        lax.fori_loop(0, N_EXPERTS, start_expert, 0)
        lax.fori_loop(n_used_ref[0], n_blocks, lambda b, c: (zero_block_copy(b).start(), c)[1], 0)

    def start(g, carry):
        for u in range(DMA_UNROLL):
            for cp in token_copies(g, u):
                cp.start()
        return carry
    lax.fori_loop(0, n_tok // DMA_UNROLL, start, 0)

    def wait(g, carry):
        for u in range(DMA_UNROLL):
            for cp in token_copies(g, u):
                cp.wait()
        return carry
    lax.fori_loop(0, n_tok // DMA_UNROLL, wait, 0)

    @pl.when(step == 0)
    def _():
        def wait_expert(e, carry):
            first, n_pad = pad_rows(e)
            lax.fori_loop(0, n_pad, lambda r, c: (zero_copy(first + r).wait(), c)[1], 0)
            return carry
        lax.fori_loop(0, N_EXPERTS, wait_expert, 0)
        lax.fori_loop(n_used_ref[0], n_blocks, lambda b, c: (zero_block_copy(b).wait(), c)[1], 0)


def _dispatch(counts, pstart, n_used, dest, xp, n_rows):
    t = xp.shape[0]
    assert MOE_TILE & (MOE_TILE - 1) == 0 and t % DISPATCH_TILE == 0 and DISPATCH_TILE % DMA_UNROLL == 0
    dest_blocks = dest.reshape(t // DISPATCH_TILE, 1, TOP_K * DISPATCH_TILE)
    grid_spec = pltpu.PrefetchScalarGridSpec(
        num_scalar_prefetch=3,
        grid=(t // DISPATCH_TILE,),
        in_specs=[
            pl.BlockSpec((1, 1, TOP_K * DISPATCH_TILE), lambda i, c, p, n: (i, 0, 0), memory_space=pltpu.SMEM),
            pl.BlockSpec((DISPATCH_TILE // DMA_UNROLL, DMA_UNROLL, PACK_W), lambda i, c, p, n: (i, 0, 0)),
        ],
        out_specs=pl.BlockSpec(memory_space=pl.ANY),
        scratch_shapes=[
            pltpu.VMEM((MOE_TILE, PACK_W), jnp.uint32),
            pltpu.SemaphoreType.DMA(()),
            pltpu.SemaphoreType.DMA(()),
        ],
    )
    return pl.pallas_call(
        _dispatch_kernel,
        grid_spec=grid_spec,
        out_shape=jax.ShapeDtypeStruct((n_rows, PACK_W), jnp.uint32),
        compiler_params=_cparams(1),
        name="dispatch",
    )(counts, pstart, n_used, dest_blocks, xp.reshape(t // DMA_UNROLL, DMA_UNROLL, PACK_W))


def _expert_kernel(blk_e_ref, n_used_ref, first_ref, slot_ref, next_e_ref, next2_e_ref, x_ref,
                   w1_hbm, w3_hbm, w2_hbm, y_ref, w1f_ref, w3f_ref, w2f_ref, sem,
                   w1b_ref, w3b_ref, w2b_ref, *, layer):
    b = pl.program_id(0)

    def weight_copies(e, s):
        return [pltpu.make_async_copy(w_hbm.at[layer, e], wf_ref.at[s], sem.at[s])
                for w_hbm, wf_ref in ((w1_hbm, w1f_ref), (w3_hbm, w3f_ref), (w2_hbm, w2f_ref))]

    def start_weights(e, s):
        for cp in weight_copies(e, s):
            cp.start()

    @pl.when(b < n_used_ref[0])
    def _():
        @pl.when(first_ref[b] == 1)
        def _():
            s = slot_ref[b]
            e = blk_e_ref[b]
            e_next = next_e_ref[e]
            e_next2 = next2_e_ref[e]

            @pl.when(b == 0)
            def _():
                start_weights(e, s)

                @pl.when(e_next >= 0)
                def _():
                    start_weights(e_next, 1 - s)

            for cp in weight_copies(e, s):
                cp.wait()
            w1b_ref[...] = w1f_ref[s].astype(BF16)
            w3b_ref[...] = w3f_ref[s].astype(BF16)
            w2b_ref[...] = w2f_ref[s].astype(BF16)

            @pl.when(e_next2 >= 0)
            def _():
                start_weights(e_next2, s)

        xb = _unpack_rows(x_ref[...]).astype(BF16)
        h1 = _dot(xb, w1b_ref[...])
        h3 = _dot(xb, w3b_ref[...])
        h = (h1 * jax.nn.sigmoid(h1)) * h3
        y_ref[...] = _pack_rows(_dot(h.astype(BF16), w2b_ref[...]))

    @pl.when(b >= n_used_ref[0])
    def _():
        y_ref[...] = jnp.zeros_like(y_ref)


def _experts(xbuf, blk_e, n_used, first, slot, next_e, next2_e, layer, w1, w3, w2):
    n_blocks = xbuf.shape[0] // MOE_TILE

    def x_index(b, be, nu, *_):
        return (jnp.minimum(b, jnp.maximum(nu[0] - 1, 0)), 0)

    grid_spec = pltpu.PrefetchScalarGridSpec(
        num_scalar_prefetch=6,
        grid=(n_blocks,),
        in_specs=[
            pl.BlockSpec((MOE_TILE, PACK_W), x_index),
            pl.BlockSpec(memory_space=pl.ANY),
            pl.BlockSpec(memory_space=pl.ANY),
            pl.BlockSpec(memory_space=pl.ANY),
        ],
        out_specs=pl.BlockSpec((MOE_TILE, PACK_W), lambda b, *_: (b, 0)),
        scratch_shapes=[
            pltpu.VMEM((2, D_MODEL, EXPERT_FF), F32),
            pltpu.VMEM((2, D_MODEL, EXPERT_FF), F32),
            pltpu.VMEM((2, EXPERT_FF, D_MODEL), F32),
            pltpu.SemaphoreType.DMA((2,)),
            pltpu.VMEM((D_MODEL, EXPERT_FF), BF16),
            pltpu.VMEM((D_MODEL, EXPERT_FF), BF16),
            pltpu.VMEM((EXPERT_FF, D_MODEL), BF16),
        ],
    )
    return pl.pallas_call(
        functools.partial(_expert_kernel, layer=layer),
        grid_spec=grid_spec,
        out_shape=jax.ShapeDtypeStruct(xbuf.shape, jnp.uint32),
        compiler_params=_cparams(1),
        name="experts",
    )(blk_e, n_used, first, slot, next_e, next2_e, xbuf, w1, w3, w2)


def _combine_kernel(dest_ref, dest_next_ref, info_ref, x_ref, y_hbm, g_ref, b_ref, o_ref, rows_ref, sem):
    i = pl.program_id(0)
    n_steps = pl.num_programs(0)
    tm = x_ref.shape[0]
    slot = lax.rem(i, 2)

    def row_copies(idx_ref, s, g, u):
        r = g * DMA_UNROLL + u
        return [pltpu.make_async_copy(y_hbm.at[pl.ds(idx_ref[0, 0, TOP_K * r + k], 1), :],
                                      rows_ref.at[s, k, g, pl.ds(u, 1), :], sem.at[s])
                for k in range(TOP_K)]

    def start_tile(idx_ref, s):
        def body(g, carry):
            for u in range(DMA_UNROLL):
                for cp in row_copies(idx_ref, s, g, u):
                    cp.start()
            return carry
        lax.fori_loop(0, tm // DMA_UNROLL, body, 0)

    @pl.when(i == 0)
    def _():
        start_tile(dest_ref, 0)

    @pl.when(i + 1 < n_steps)
    def _():
        start_tile(dest_next_ref, 1 - slot)

    def wait(g, carry):
        for u in range(DMA_UNROLL):
            for cp in row_copies(dest_ref, slot, g, u):
                cp.wait()
        return carry
    lax.fori_loop(0, tm // DMA_UNROLL, wait, 0)

    info = info_ref[...]
    f = None
    for k in range(TOP_K):
        rows = rows_ref[slot, k].reshape(tm, PACK_W)
        term = info[:, 4 + k:5 + k] * _unpack_rows(rows)
        f = term if f is None else f + term
    y = DN_ALPHA * x_ref[...] + f
    o_ref[...] = _layer_norm(y, g_ref[...], b_ref[...])


def _combine(dest, info, x, yb, ln_g, ln_b):
    t = x.shape[0]
    n_steps = t // ROW_TILE
    assert ROW_TILE % DMA_UNROLL == 0
    dest_blocks = dest.reshape(n_steps, 1, TOP_K * ROW_TILE)
    idx_block = (1, 1, TOP_K * ROW_TILE)
    return pl.pallas_call(
        _combine_kernel,
        grid=(n_steps,),
        in_specs=[
            pl.BlockSpec(idx_block, lambda i: (i, 0, 0), memory_space=pltpu.SMEM),
            pl.BlockSpec(idx_block, lambda i: (jnp.minimum(i + 1, n_steps - 1), 0, 0), memory_space=pltpu.SMEM),
            pl.BlockSpec((ROW_TILE, LANES), lambda i: (i, 0)),
            pl.BlockSpec((ROW_TILE, D_MODEL), lambda i: (i, 0)),
            pl.BlockSpec(memory_space=pl.ANY),
            _resident((1, D_MODEL)),
            _resident((1, D_MODEL)),
        ],
        out_specs=pl.BlockSpec((ROW_TILE, D_MODEL), lambda i: (i, 0)),
        out_shape=jax.ShapeDtypeStruct((t, D_MODEL), F32),
        scratch_shapes=[
            pltpu.VMEM((2, TOP_K, ROW_TILE // DMA_UNROLL, DMA_UNROLL, PACK_W), jnp.uint32),
            pltpu.SemaphoreType.DMA((2,)),
        ],
        compiler_params=_cparams(1),
        name="combine",
    )(dest_blocks, dest_blocks, info, x, yb, ln_g, ln_b)


def _plan_kernel(cnt_ref, pstart_ref, n_used_ref, blk_e_ref, first_ref, slot_ref, next_ref, next2_ref):
    n_blocks = blk_e_ref.shape[0]
    shift = MOE_TILE.bit_length() - 1

    def per_expert(e, carry):
        block, ordinal = carry
        n_blk = (cnt_ref[e] + (MOE_TILE - 1)) >> shift
        pstart_ref[e] = block << shift

        def per_block(j, c):
            blk_e_ref[block + j] = e
            first_ref[block + j] = jnp.where(j == 0, 1, 0)
            slot_ref[block + j] = ordinal & 1
            return c
        lax.fori_loop(0, n_blk, per_block, 0)
        return block + n_blk, ordinal + jnp.where(n_blk > 0, 1, 0)
    n_used, _ = lax.fori_loop(0, N_EXPERTS, per_expert, (jnp.int32(0), jnp.int32(0)))
    n_used_ref[0] = n_used

    def unused(b, c):
        blk_e_ref[b] = N_EXPERTS - 1
        first_ref[b] = 0
        slot_ref[b] = 0
        return c
    lax.fori_loop(n_used, n_blocks, unused, 0)

    def backwards(k, carry):
        nxt, nxt2 = carry
        e = N_EXPERTS - 1 - k
        next_ref[e] = nxt
        next2_ref[e] = nxt2
        used = cnt_ref[e] > 0
        return jnp.where(used, e, nxt), jnp.where(used, nxt, nxt2)
    lax.fori_loop(0, N_EXPERTS, backwards, (jnp.int32(-1), jnp.int32(-1)))


def _plan(counts, n_blocks):
    smem = pl.BlockSpec(memory_space=pltpu.SMEM)
    i32 = jnp.int32
    return pl.pallas_call(
        _plan_kernel,
        in_specs=[smem],
        out_specs=[smem] * 7,
        out_shape=[
            jax.ShapeDtypeStruct((N_EXPERTS,), i32),
            jax.ShapeDtypeStruct((1,), i32),
            jax.ShapeDtypeStruct((n_blocks,), i32),
            jax.ShapeDtypeStruct((n_blocks,), i32),
            jax.ShapeDtypeStruct((n_blocks,), i32),
            jax.ShapeDtypeStruct((N_EXPERTS,), i32),
            jax.ShapeDtypeStruct((N_EXPERTS,), i32),
        ],
        name="moe_plan",
    )(counts)


def _moe(x, xp, wg1, bg1, wg2, bg2, layer, w1, w3, w2, ln_g, ln_b):
    t = x.shape[0]
    w_r = jnp.concatenate([wg1, jnp.transpose(wg2, (1, 0, 2)).reshape(D_MODEL, N_EXPERTS)], axis=1)
    w_r = jnp.pad(w_r, ((0, 0), (0, LANES - w_r.shape[1])))
    b_r = jnp.pad(jnp.concatenate([bg1, bg2.reshape(-1)]), (0, LANES - N_GROUPS - N_EXPERTS))[None, :]
    info, cnt = _router(x, w_r, b_r)

    expert = info[:, 0:2].astype(jnp.int32)
    rank = info[:, 2:4].astype(jnp.int32)
    counts = cnt[0, ROUTER_LANE0:ROUTER_LANE0 + N_EXPERTS].astype(jnp.int32)
    n_blocks = (t * TOP_K) // MOE_TILE + N_EXPERTS
    pstart, n_used, blk_e, first, slot, next_e, next2_e = _plan(counts, n_blocks)
    experts = jnp.arange(N_EXPERTS, dtype=jnp.int32)
    dest = jnp.sum(jnp.where(expert[:, :, None] == experts, pstart, 0), axis=-1) + rank

    xbuf = _dispatch(counts, pstart, n_used, dest, xp, n_blocks * MOE_TILE)
    yb = _experts(xbuf, blk_e, n_used, first, slot, next_e, next2_e, layer, w1, w3, w2)
    return _combine(dest, info, x, yb, ln_g, ln_b)


def kernel(x, mem, a_w_in, a_sgu_ln_g, a_sgu_ln_b, a_ws, a_bs, a_w_out, b_w_in, b_lambda, b_subln_g, b_w_out, shared_w_kv, rel_bias, mem_w_kv, ln_g, ln_b, moe_wg1, moe_bg1, moe_wg2, moe_bg2, moe_w1, moe_w3, moe_w2):
    b_, s_, d_ = x.shape
    assert b_ == 1 and d_ == D_MODEL
    h = x.reshape(s_, d_)
    kt_mem, v_mem = _memkv(mem.reshape(MEM_TOKENS, d_), mem_w_kv)
    bias_table = _near_bias_tables(rel_bias)
    shared_k = shared_v = None
    for l in range(DEPTH):
        if l < N_A_LAYERS:
            i = l
            mix, qm = _pre_a(h, a_w_in[i].astype(BF16), a_sgu_ln_g[i][None, :], a_sgu_ln_b[i][None, :],
                             a_ws[i], jnp.transpose(a_bs[i]))
            w_out = a_w_out[i]
        else:
            i = l - N_A_LAYERS
            qd, qm = _pre_b(h, jnp.transpose(b_w_in[i][:, :DIFF_QK_W]).astype(BF16),
                            b_w_in[i][:, DIFF_QK_W:].astype(BF16))
            lambda_init = 0.8 - 0.6 * math.exp(-0.3 * l)
            mix = _attention(qd, shared_k, shared_v, bias_table, b_lambda[i], b_subln_g[i][None, :],
                             lambda_init)
            w_out = b_w_out[i]
        h, hp = _post(mix, qm, kt_mem, v_mem, l, h, w_out.astype(BF16), ln_g[l, 0][None, :], ln_b[l, 0][None, :])
        h = _moe(h, hp, moe_wg1[l], moe_bg1[l], moe_wg2[l], moe_bg2[l], l, moe_w1, moe_w3, moe_w2,
                 ln_g[l, 1][None, :], ln_b[l, 1][None, :])
        if l == N_A_LAYERS - 1:
            shared_k, shared_v = _kvproj(h, shared_w_kv[:, :DIFF_QK_W].astype(BF16),
                                         jnp.transpose(shared_w_kv[:, DIFF_QK_W:]).astype(BF16))
    return h.reshape(b_, s_, d_)
```
